```python
import math
import jax, jax.numpy as jnp
from jax import lax
import numpy as np

D_MODEL = 1024
BATCH = 4
SEQ = 4096
DEPTH = 4
DEC_BATCH = 32
DEC_SEQ = 1
PAST_LEN = 8192
PAGE_SIZE = 128

N_MIXERS = 2
N_NSA_LAYERS = (DEPTH + 1) // 2
N_DIFF_LAYERS = DEPTH // 2
Q_BLOCK = 128
N_BUCKETS = 32
MAX_DISTANCE = 128
NSA_HEADS = 16
NSA_GROUPS = 4
NSA_HPG = NSA_HEADS // NSA_GROUPS
NSA_HD = D_MODEL // NSA_HEADS
CMP_BLOCK = 32
CMP_STRIDE = 16
CMP_HIDDEN = 2 * NSA_HD
SEL_BLOCK = 64
SEL_TOPK = 16
WINDOW = 512
FORCE_BONUS = 1e4
NSA_KV_COLS = NSA_GROUPS * NSA_HD
NSA_Q_COLS = NSA_HEADS * NSA_HD
NSA_IN_COLS = NSA_Q_COLS + 6 * NSA_KV_COLS + 3 * NSA_HEADS
DIFF_HEADS = 8
DIFF_D = D_MODEL // (2 * DIFF_HEADS)
DIFF_VD = 2 * DIFF_D
DIFF_Q_COLS = DIFF_HEADS * DIFF_VD
DIFF_IN_COLS = 3 * DIFF_HEADS * DIFF_VD
N_BIAS_COLS = NSA_HEADS
D_FF = 4 * D_MODEL
ALPHA = (2 * DEPTH) ** 0.25
BETA = (8 * DEPTH) ** -0.25
LN_EPS = 1e-5
NEG = -1e30

kernel_name = "hybrid_nsa_diffattn_decoder_step"


def layer_norm(x, g, b):
    xf = x.astype(jnp.float32)
    mu = jnp.mean(xf, -1, keepdims=True)
    var = jnp.mean(jnp.square(xf - mu), -1, keepdims=True)
    return ((xf - mu) * lax.rsqrt(var + LN_EPS) * g + b).astype(x.dtype)


def rms_norm(x, g):
    xf = x.astype(jnp.float32)
    return (xf * lax.rsqrt(jnp.mean(xf * xf, -1, keepdims=True) + LN_EPS) * g).astype(x.dtype)


def t5_bucket(dist):
    n = jnp.maximum(dist, 0)
    max_exact = N_BUCKETS // 2
    large = max_exact + (jnp.log(jnp.maximum(n, 1).astype(jnp.float32) / max_exact)
                         / math.log(MAX_DISTANCE / max_exact) * (N_BUCKETS - max_exact)).astype(jnp.int32)
    return jnp.where(n < max_exact, n, jnp.minimum(large, N_BUCKETS - 1))


def masked_softmax(s, mask):
    s = jnp.where(mask, s, NEG)
    m = jnp.max(s, axis=-1, keepdims=True)
    e = jnp.where(mask, jnp.exp(s - m), 0.0)
    return e / jnp.maximum(jnp.sum(e, -1, keepdims=True), 1e-30)


def map_query_blocks(fn, *arrs):
    tq = arrs[0].shape[1]
    qb = Q_BLOCK if tq % Q_BLOCK == 0 else tq
    nb = tq // qb
    xs = tuple(jnp.moveaxis(a.reshape(a.shape[0], nb, qb, *a.shape[2:]), 1, 0) for a in arrs)
    out = lax.map(lambda args: fn(args[0] * qb, *args[1:]), (jnp.arange(nb, dtype=jnp.int32),) + xs)
    out = jnp.moveaxis(out, 0, 1)
    return out.reshape(out.shape[0], tq, *out.shape[3:])


def compress_rows(rows, pos_emb, w1, w2):
    bsz, t = rows.shape[:2]
    n_cmp = (t - CMP_BLOCK) // CMP_STRIDE + 1
    n_sub = CMP_BLOCK // CMP_STRIDE
    rows = jnp.pad(rows, ((0, 0), (0, (-t) % CMP_STRIDE), (0, 0), (0, 0)))
    chunks = rows.reshape(bsz, -1, CMP_STRIDE, NSA_GROUPS, NSA_HD)
    w1r = w1.reshape(n_sub, CMP_STRIDE, NSA_HD, CMP_HIDDEN)
    z = jnp.einsum('bcsgh,jshk->bcjgk', chunks, w1r)
    hid = pos_emb.reshape(-1) @ w1
    for j in range(n_sub):
        hid = hid + z[:, j:j + n_cmp, j]
    return jax.nn.gelu(hid) @ w2


def nsa_mixer(x, past_kv, win_buf, w_in, w_out, cmp_pos, cmp_w1, cmp_w2, rel_bias):
    bsz, s = x.shape[:2]
    p_len, wb = past_kv.shape[1], win_buf.shape[1]
    proj = x @ w_in
    q = proj[..., :NSA_Q_COLS].reshape(bsz, s, NSA_GROUPS, NSA_HPG, NSA_HD)
    kv = proj[..., NSA_Q_COLS:NSA_Q_COLS + 6 * NSA_KV_COLS].reshape(bsz, s, 6, NSA_GROUPS, NSA_HD)
    gates = jax.nn.sigmoid(proj[..., NSA_Q_COLS + 6 * NSA_KV_COLS:].astype(jnp.float32))
    gates = gates.reshape(bsz, s, 3, NSA_GROUPS, NSA_HPG)
    new_kv, new_win = kv[:, :, :4], kv[:, :, 4:]
    rows = jnp.concatenate([past_kv, new_kv], axis=1)
    t = rows.shape[1]
    k_cmp = compress_rows(rows[:, :, 0], cmp_pos[0], cmp_w1[0], cmp_w2[0])
    v_cmp = compress_rows(rows[:, :, 1], cmp_pos[1], cmp_w1[1], cmp_w2[1])
    n_cmp = k_cmp.shape[1]
    cmp_end = jnp.arange(n_cmp) * CMP_STRIDE + CMP_BLOCK - 1
    n_sel = -(-t // SEL_BLOCK)
    top = min(SEL_TOPK, n_sel)
    sel = jnp.pad(rows[:, :, 2:], ((0, 0), (0, n_sel * SEL_BLOCK - t), (0, 0), (0, 0), (0, 0)))
    sel = sel.reshape(bsz, n_sel, SEL_BLOCK, 2, NSA_GROUPS, NSA_HD).transpose(3, 0, 4, 1, 2, 5)
    k_sel, v_sel = sel[0], sel[1]
    c_start = jnp.arange(n_cmp)[:, None] * CMP_STRIDE
    s_start = jnp.arange(n_sel)[None, :] * SEL_BLOCK
    cover = ((c_start < s_start + SEL_BLOCK) & (c_start + CMP_BLOCK > s_start)).astype(jnp.float32)
    win = jnp.concatenate([win_buf, new_win], axis=1)
    win_pos0 = p_len - wb
    win_pad = jnp.pad(win, ((0, 0), (WINDOW, 0), (0, 0), (0, 0), (0, 0)))
    bias_table = rel_bias.reshape(N_BUCKETS, NSA_GROUPS, NSA_HPG)
    scale = NSA_HD ** -0.5
    b_idx = jnp.arange(bsz)[:, None, None, None]
    g_idx = jnp.arange(NSA_GROUPS)[None, None, :, None]

    def block(off, q_b, g_b):
        qb = q_b.shape[1]
        q_pos = p_len + off + jnp.arange(qb)
        s_c = jnp.einsum('bqgih,bngh->bqgin', q_b, k_cmp).astype(jnp.float32) * scale
        p_c = masked_softmax(s_c, (cmp_end[None, :] <= q_pos[:, None])[None, :, None, None, :])
        o_c = jnp.einsum('bqgin,bngh->bqgih', p_c.astype(x.dtype), v_cmp)
        imp = jnp.einsum('bqgn,nj->bqgj', jnp.sum(p_c, 3), cover)
        j = jnp.arange(n_sel)[None, :]
        cur = (q_pos // SEL_BLOCK)[:, None]
        valid = (j <= cur)[None, :, None, :]
        forced = ((j == 0) | (j == cur) | (j == cur - 1))[None, :, None, :]
        score = jnp.where(valid, imp + jnp.where(forced, FORCE_BONUS, 0.0), NEG)
        vals, idx = lax.top_k(score, top)
        ok = vals > NEG / 2
        ks_g = k_sel[b_idx, g_idx, idx]
        vs_g = v_sel[b_idx, g_idx, idx]
        k_pos = idx[..., None] * SEL_BLOCK + jnp.arange(SEL_BLOCK)
        dist = q_pos[None, :, None, None, None] - k_pos
        m_s = ok[..., None] & (dist >= 0)
        bias_s = jnp.moveaxis(bias_table[t5_bucket(dist), g_idx[..., None]], -1, 3)
        s_s = jnp.einsum('bqgih,bqgnlh->bqginl', q_b, ks_g).astype(jnp.float32) * scale + bias_s
        p_s = masked_softmax(s_s.reshape(*s_s.shape[:4], -1),
                             m_s.reshape(bsz, qb, NSA_GROUPS, 1, -1)).reshape(s_s.shape)
        o_s = jnp.einsum('bqginl,bqgnlh->bqgih', p_s.astype(x.dtype), vs_g)
        kw = lax.dynamic_slice_in_dim(win_pad, wb + off, WINDOW + qb, axis=1)
        kw_pos = p_len + off - WINDOW + jnp.arange(WINDOW + qb)
        dist_w = q_pos[:, None] - kw_pos[None, :]
        m_w = ((kw_pos >= win_pos0)[None, :] & (dist_w >= 0) & (dist_w <= WINDOW))[None, :, None, None, :]
        bias_w = jnp.transpose(bias_table[t5_bucket(dist_w)], (0, 2, 3, 1))
        s_w = jnp.einsum('bqgih,bkgh->bqgik', q_b, kw[:, :, 0]).astype(jnp.float32) * scale + bias_w[None]
        p_w = masked_softmax(s_w, m_w)
        o_w = jnp.einsum('bqgik,bkgh->bqgih', p_w.astype(x.dtype), kw[:, :, 1])
        o = (g_b[:, :, 0][..., None] * o_c + g_b[:, :, 1][..., None] * o_s + g_b[:, :, 2][..., None] * o_w)
        return o.astype(x.dtype)

    o = map_query_blocks(block, q, gates)
    y = o.reshape(bsz, s, NSA_Q_COLS) @ w_out
    keep = min(WINDOW, win.shape[1])
    return y, new_kv, win[:, win.shape[1] - keep:]


def diff_mixer(x, past_kv, w_in, w_out, lam, subln_g, rel_bias, lambda_init):
    bsz, s = x.shape[:2]
    p_len = past_kv.shape[1]
    proj = x @ w_in
    q = proj[..., :DIFF_Q_COLS].reshape(bsz, s, DIFF_HEADS, 2, DIFF_D)
    new_kv = proj[..., DIFF_Q_COLS:].reshape(bsz, s, 2, DIFF_HEADS, DIFF_VD)
    rows = jnp.concatenate([past_kv, new_kv], axis=1)
    t = rows.shape[1]
    k = rows[:, :, 0].reshape(bsz, t, DIFF_HEADS, 2, DIFF_D)
    v = rows[:, :, 1]
    k_pos = jnp.arange(t)
    lf = lam.astype(jnp.float32)
    lam_full = jnp.exp(jnp.sum(lf[0] * lf[1])) - jnp.exp(jnp.sum(lf[2] * lf[3])) + lambda_init
    bias_table = rel_bias.reshape(N_BUCKETS, DIFF_HEADS, 2)
    scale = DIFF_D ** -0.5

    def block(off, q_b):
        qb = q_b.shape[1]
        q_pos = p_len + off + jnp.arange(qb)
        dist = q_pos[:, None] - k_pos[None, :]
        bias = jnp.transpose(bias_table[t5_bucket(dist)], (0, 2, 3, 1))
        sc = jnp.einsum('bqhcd,bkhcd->bqhck', q_b, k).astype(jnp.float32) * scale + bias[None]
        p = masked_softmax(sc, (dist >= 0)[None, :, None, None, :])
        a = p[:, :, :, 0] - lam_full * p[:, :, :, 1]
        return jnp.einsum('bqhk,bkhe->bqhe', a.astype(x.dtype), v)

    o = map_query_blocks(block, q)
    o = rms_norm(o, subln_g) * (1.0 - lambda_init)
    return o.reshape(bsz, s, DIFF_HEADS * DIFF_VD) @ w_out, new_kv


def squared_relu_mlp(x, w_up, w_down):
    return jnp.square(jax.nn.relu(x @ w_up)) @ w_down


def post_norm(x, sub, g, b):
    return layer_norm(ALPHA * x + sub, g, b)


def setup_inputs(seed: int = 0) -> dict:
    key = jax.random.key(seed)
    ks = jax.random.split(key, 20)
    n_pages = PAST_LEN // PAGE_SIZE
    n_used = DEC_BATCH * n_pages
    n_pool = n_used + max(1, n_used // 4)
    wbuf = min(WINDOW, PAST_LEN)

    def nrm(k, shape, sc):
        return jax.random.normal(k, shape, jnp.float32) * sc

    page_table = jax.random.permutation(ks[3], n_pool)[:n_used].reshape(DEC_BATCH, n_pages).astype(jnp.int32)
    return {
        "x_prompt": nrm(ks[0], (BATCH, SEQ, D_MODEL), 1.0),
        "x_sample": nrm(ks[1], (DEC_BATCH, DEC_SEQ, D_MODEL), 1.0),
        "cache_nsa_kv": nrm(ks[2], (N_NSA_LAYERS, n_pool, PAGE_SIZE, 4, NSA_GROUPS, NSA_HD), 1.0),
        "state_nsa_win": nrm(ks[4], (N_NSA_LAYERS, DEC_BATCH, wbuf, 2, NSA_GROUPS, NSA_HD), 1.0),
        "cache_diff_kv": nrm(ks[5], (N_DIFF_LAYERS, n_pool, PAGE_SIZE, 2, DIFF_HEADS, DIFF_VD), 1.0),
        "page_table": page_table,
        "rel_bias": nrm(ks[6], (N_BUCKETS, N_BIAS_COLS), 0.5),
        "nsa_w_in": nrm(ks[7], (N_NSA_LAYERS, D_MODEL, NSA_IN_COLS), D_MODEL ** -0.5),
        "nsa_w_out": nrm(ks[8], (N_NSA_LAYERS, NSA_Q_COLS, D_MODEL), NSA_Q_COLS ** -0.5 * BETA),
        "nsa_cmp_pos": nrm(ks[9], (N_NSA_LAYERS, 2, CMP_BLOCK, NSA_HD), 0.1),
        "nsa_cmp_w1": nrm(ks[10], (N_NSA_LAYERS, 2, CMP_BLOCK * NSA_HD, CMP_HIDDEN), (CMP_BLOCK * NSA_HD) ** -0.5),
        "nsa_cmp_w2": nrm(ks[11], (N_NSA_LAYERS, 2, CMP_HIDDEN, NSA_HD), CMP_HIDDEN ** -0.5),
        "diff_w_in": nrm(ks[12], (N_DIFF_LAYERS, D_MODEL, DIFF_IN_COLS), D_MODEL ** -0.5),
        "diff_w_out": nrm(ks[13], (N_DIFF_LAYERS, DIFF_HEADS * DIFF_VD, D_MODEL), (DIFF_HEADS * DIFF_VD) ** -0.5 * BETA),
        "diff_lambda": nrm(ks[14], (N_DIFF_LAYERS, 4, DIFF_D), 0.1),
        "diff_subln_g": 1.0 + nrm(ks[15], (N_DIFF_LAYERS, DIFF_VD), 0.02),
        "mlp_w_up": nrm(ks[16], (DEPTH, D_MODEL, D_FF), D_MODEL ** -0.5),
        "mlp_w_down": nrm(ks[17], (DEPTH, D_FF, D_MODEL), D_FF ** -0.5 * BETA),
        "ln_g": 1.0 + nrm(ks[18], (DEPTH, 2, D_MODEL), 0.02),
        "ln_b": nrm(ks[19], (DEPTH, 2, D_MODEL), 0.02),
    }


def reference(x_prompt, x_sample, cache_nsa_kv, state_nsa_win, cache_diff_kv, page_table, rel_bias,
              nsa_w_in, nsa_w_out, nsa_cmp_pos, nsa_cmp_w1, nsa_cmp_w2,
              diff_w_in, diff_w_out, diff_lambda, diff_subln_g,
              mlp_w_up, mlp_w_down, ln_g, ln_b):
    xp, xs = x_prompt, x_sample
    bp, bs = xp.shape[0], xs.shape[0]
    past_len = page_table.shape[1] * cache_nsa_kv.shape[2]
    nsa_kv_p, nsa_kv_s, nsa_win_p, nsa_win_s, diff_kv_p, diff_kv_s = [], [], [], [], [], []
    for i in range(DEPTH):
        li = i // N_MIXERS
        if i % N_MIXERS == 0:
            w = (nsa_w_in[li], nsa_w_out[li], nsa_cmp_pos[li], nsa_cmp_w1[li], nsa_cmp_w2[li], rel_bias)
            past_p = jnp.zeros((bp, 0, 4, NSA_GROUPS, NSA_HD), xp.dtype)
            win_p = jnp.zeros((bp, 0, 2, NSA_GROUPS, NSA_HD), xp.dtype)
            past_s = cache_nsa_kv[li, page_table].reshape(bs, past_len, 4, NSA_GROUPS, NSA_HD)
            yp, kvp, wp = nsa_mixer(xp, past_p, win_p, *w)
            ys, kvs, wsn = nsa_mixer(xs, past_s, state_nsa_win[li], *w)
            nsa_kv_p.append(kvp)
            nsa_kv_s.append(kvs)
            nsa_win_p.append(wp)
            nsa_win_s.append(wsn)
        else:
            lambda_init = 0.8 - 0.6 * math.exp(-0.3 * i)
            w = (diff_w_in[li], diff_w_out[li], diff_lambda[li], diff_subln_g[li], rel_bias, lambda_init)
            past_p = jnp.zeros((bp, 0, 2, DIFF_HEADS, DIFF_VD), xp.dtype)
            past_s = cache_diff_kv[li, page_table].reshape(bs, past_len, 2, DIFF_HEADS, DIFF_VD)
            yp, kvp = diff_mixer(xp, past_p, *w)
            ys, kvs = diff_mixer(xs, past_s, *w)
            diff_kv_p.append(kvp)
            diff_kv_s.append(kvs)
        xp = post_norm(xp, yp, ln_g[i, 0], ln_b[i, 0])
        xs = post_norm(xs, ys, ln_g[i, 0], ln_b[i, 0])
        xp = post_norm(xp, squared_relu_mlp(xp, mlp_w_up[i], mlp_w_down[i]), ln_g[i, 1], ln_b[i, 1])
        xs = post_norm(xs, squared_relu_mlp(xs, mlp_w_up[i], mlp_w_down[i]), ln_g[i, 1], ln_b[i, 1])
    return (xp, xs, jnp.stack(nsa_kv_p), jnp.stack(nsa_kv_s), jnp.stack(nsa_win_p), jnp.stack(nsa_win_s),
            jnp.stack(diff_kv_p), jnp.stack(diff_kv_s))
```

```python
import functools
import math

import jax
import jax.numpy as jnp
from jax import lax
from jax.experimental import pallas as pl
from jax.experimental.pallas import tpu as pltpu

F32 = jnp.float32
BF16 = jnp.bfloat16

D_MODEL = 1024
DEPTH = 4
N_MIXERS = 2
Q_BLOCK = 128
N_BUCKETS = 32
MAX_DISTANCE = 128
NSA_HEADS = 16
NSA_GROUPS = 4
NSA_HPG = NSA_HEADS // NSA_GROUPS
NSA_HD = D_MODEL // NSA_HEADS
CMP_BLOCK = 32
CMP_STRIDE = 16
CMP_HIDDEN = 2 * NSA_HD
SEL_BLOCK = 64
SEL_TOPK = 16
WINDOW = 512
FORCE_BONUS = 1e4
NSA_KV_COLS = NSA_GROUPS * NSA_HD
NSA_Q_COLS = NSA_HEADS * NSA_HD
DIFF_HEADS = 8
DIFF_D = D_MODEL // (2 * DIFF_HEADS)
DIFF_VD = 2 * DIFF_D
DIFF_Q_COLS = DIFF_HEADS * DIFF_VD
D_FF = 4 * D_MODEL
ALPHA = (2 * DEPTH) ** 0.25
LN_EPS = 1e-5
NEG = -1e30
HALF_NEG = -5e29

LANES = 128
KEY_CHUNK = 128
VMEM_LIMIT = 56 * 1024 * 1024

_NT = (((1,), (1,)), ((), ()))
_TN = (((0,), (0,)), ((), ()))


def _params(sem, vmem=VMEM_LIMIT):
    return pltpu.CompilerParams(dimension_semantics=sem, vmem_limit_bytes=vmem)


def _linear_kernel(a_ref, w_ref, *o_refs, act):
    y = jnp.dot(a_ref[...].astype(BF16), w_ref[...], preferred_element_type=F32)
    if act == "sigmoid":
        y = jax.nn.sigmoid(y)
    for o_ref in o_refs:
        o_ref[...] = y.astype(o_ref.dtype)


def linear(a, w, out_dtypes, act=None, tm=512, tn=512):
    m, k = a.shape
    n = w.shape[1]
    tm, tn = min(tm, m), min(tn, n)
    assert m % tm == 0 and n % tn == 0
    outs = pl.pallas_call(
        functools.partial(_linear_kernel, act=act),
        out_shape=[jax.ShapeDtypeStruct((m, n), dt) for dt in out_dtypes],
        grid=(m // tm, n // tn),
        in_specs=[pl.BlockSpec((tm, k), lambda i, j: (i, 0)),
                  pl.BlockSpec((k, tn), lambda i, j: (0, j))],
        out_specs=[pl.BlockSpec((tm, tn), lambda i, j: (i, j)) for _ in out_dtypes],
        compiler_params=_params(("parallel", "parallel")),
        name="linear",
    )(a, w)
    return outs


def _post_norm(resid, sub, g, b):
    z = ALPHA * resid + sub
    mu = jnp.mean(z, axis=-1, keepdims=True)
    zc = z - mu
    var = jnp.mean(zc * zc, axis=-1, keepdims=True)
    return zc * lax.rsqrt(var + LN_EPS) * g + b


def _linear_ln_kernel(a_ref, w_ref, r_ref, g_ref, b_ref, o_ref):
    y = jnp.dot(a_ref[...].astype(BF16), w_ref[...], preferred_element_type=F32)
    o_ref[...] = _post_norm(r_ref[...], y, g_ref[...], b_ref[...])


def linear_post_norm(a, w, resid, g, b, tm=512):
    m, k = a.shape
    n = w.shape[1]
    tm = min(tm, m)
    assert m % tm == 0
    return pl.pallas_call(
        _linear_ln_kernel,
        out_shape=jax.ShapeDtypeStruct((m, n), F32),
        grid=(m // tm,),
        in_specs=[pl.BlockSpec((tm, k), lambda i: (i, 0)),
                  pl.BlockSpec((k, n), lambda i: (0, 0)),
                  pl.BlockSpec((tm, n), lambda i: (i, 0)),
                  pl.BlockSpec((1, n), lambda i: (0, 0)),
                  pl.BlockSpec((1, n), lambda i: (0, 0))],
        out_specs=pl.BlockSpec((tm, n), lambda i: (i, 0)),
        compiler_params=_params(("parallel",)),
        name="linear_post_norm",
    )(a, w, resid, g.reshape(1, n), b.reshape(1, n))


def _mlp_kernel(x_ref, wu_ref, wd_ref, g_ref, b_ref, o_ref, acc_ref):
    f = pl.program_id(1)

    @pl.when(f == 0)
    def _():
        acc_ref[...] = jnp.zeros_like(acc_ref)

    h = jnp.dot(x_ref[...].astype(BF16), wu_ref[...], preferred_element_type=F32)
    h = jnp.square(jnp.maximum(h, 0.0))
    acc_ref[...] += jnp.dot(h.astype(BF16), wd_ref[...], preferred_element_type=F32)

    @pl.when(f == pl.num_programs(1) - 1)
    def _():
        o_ref[...] = _post_norm(x_ref[...], acc_ref[...], g_ref[...], b_ref[...])


def mlp_post_norm(x, w_up, w_down, g, b, tm=512, tf=512):
    m, d = x.shape
    ff = w_up.shape[1]
    tm = min(tm, m)
    assert m % tm == 0 and ff % tf == 0
    return pl.pallas_call(
        _mlp_kernel,
        out_shape=jax.ShapeDtypeStruct((m, d), F32),
        grid=(m // tm, ff // tf),
        in_specs=[pl.BlockSpec((tm, d), lambda i, f: (i, 0)),
                  pl.BlockSpec((d, tf), lambda i, f: (0, f)),
                  pl.BlockSpec((tf, d), lambda i, f: (f, 0)),
                  pl.BlockSpec((1, d), lambda i, f: (0, 0)),
                  pl.BlockSpec((1, d), lambda i, f: (0, 0))],
        out_specs=pl.BlockSpec((tm, d), lambda i, f: (i, 0)),
        scratch_shapes=[pltpu.VMEM((tm, d), F32)],
        compiler_params=_params(("parallel", "arbitrary")),
        name="mlp_post_norm",
    )(x, w_up, w_down, g.reshape(1, d), b.reshape(1, d))


def _t5_bucket(dist):
    n = jnp.maximum(dist, 0)
    max_exact = N_BUCKETS // 2
    large = max_exact + (jnp.log(jnp.maximum(n, 1).astype(F32) / max_exact)
                         / math.log(MAX_DISTANCE / max_exact) * (N_BUCKETS - max_exact)).astype(jnp.int32)
    return jnp.where(n < max_exact, n, jnp.minimum(large, N_BUCKETS - 1))


def _bias_tiles(rel_bias):
    r = jnp.arange(KEY_CHUNK)[:, None]
    c = jnp.arange(KEY_CHUNK)[None, :]
    far = rel_bias[N_BUCKETS - 1]
    t_diag = jnp.moveaxis(rel_bias[_t5_bucket(r - c)], -1, 0) - far[:, None, None]
    t_prev = jnp.moveaxis(rel_bias[_t5_bucket(KEY_CHUNK + r - c)], -1, 0) - far[:, None, None]
    zero = jnp.zeros_like(t_diag)
    diag = jnp.where((c <= r)[None], t_diag, NEG)
    edge = jnp.where((c >= r)[None], zero, NEG)
    return jnp.stack([zero, t_prev, diag]), jnp.stack([diag, t_prev, zero, zero, edge])


def _key_bias(rel_bias, q_pos, k_pos):
    return rel_bias[_t5_bucket(q_pos - k_pos)]


def _softmax_step(s, v, m, l, acc):
    m_new = jnp.maximum(m, jnp.max(s, axis=-1, keepdims=True))
    p = jnp.where(s > HALF_NEG, jnp.exp(s - m_new), 0.0)
    a = jnp.exp(m - m_new)
    l = a * l + jnp.sum(p, axis=-1, keepdims=True)
    acc = a * acc + jnp.dot(p.astype(BF16), v, preferred_element_type=F32)
    return m_new, l, acc


def _compress_pair(x_ref, n_chunk, w1_ref, pos_term, w2):
    acc = jnp.zeros((n_chunk, 4 * CMP_HIDDEN), F32)
    for s in range(CMP_STRIDE):
        xs = x_ref[pl.ds(s, n_chunk, stride=CMP_STRIDE), :]
        acc = acc + jnp.dot(xs.astype(BF16), w1_ref[s], preferred_element_type=F32)
    outs = []
    for g2 in range(2):
        first = acc[:, g2 * 2 * CMP_HIDDEN: g2 * 2 * CMP_HIDDEN + CMP_HIDDEN]
        second = acc[:, g2 * 2 * CMP_HIDDEN + CMP_HIDDEN: (g2 + 1) * 2 * CMP_HIDDEN]
        second_next = jnp.concatenate([second[1:], second[:1]], axis=0)
        hid = first + second_next + pos_term
        outs.append(jnp.dot(jax.nn.gelu(hid).astype(BF16), w2, preferred_element_type=F32))
    return outs


def _compress_weights(pos, w1, w2):
    n_sub = CMP_BLOCK // CMP_STRIDE
    w1r = w1.reshape(2, n_sub, CMP_STRIDE, NSA_HD, CMP_HIDDEN)
    w1t = jnp.transpose(w1r, (0, 2, 3, 1, 4)).reshape(2, CMP_STRIDE, NSA_HD, n_sub * CMP_HIDDEN)
    zeros = jnp.zeros_like(w1t)
    w1p = jnp.concatenate([jnp.concatenate([w1t, zeros], -1), jnp.concatenate([zeros, w1t], -1)], axis=2)
    posb = jnp.broadcast_to(pos.reshape(2, 1, CMP_BLOCK * NSA_HD), (2, 8, CMP_BLOCK * NSA_HD))
    return w1p.astype(BF16), posb.astype(BF16), w1.astype(BF16), w2.astype(BF16)


def _pos_term(posb_ref, w1_ref):
    return jnp.dot(posb_ref[...], w1_ref[...], preferred_element_type=F32)[0:1]


def _compress_prompt_kernel(x_ref, w1p_ref, posb_ref, w1_ref, w2_ref, o_ref, *, n_chunk):
    pos_term = _pos_term(posb_ref.at[0], w1_ref.at[0])
    outs = _compress_pair(x_ref.at[0], n_chunk, w1p_ref.at[0], pos_term, w2_ref[0])
    for g2 in range(2):
        o_ref[0, 0, g2] = outs[g2].astype(o_ref.dtype)


def compress_prompt(kv, cmp_w):
    w1p, posb, w1, w2 = cmp_w
    bsz, s, _ = kv.shape
    n_chunk = s // CMP_STRIDE
    n_pair = NSA_GROUPS // 2
    return pl.pallas_call(
        functools.partial(_compress_prompt_kernel, n_chunk=n_chunk),
        out_shape=jax.ShapeDtypeStruct((2, bsz, NSA_GROUPS, n_chunk, NSA_HD), BF16),
        grid=(2, bsz, n_pair),
        in_specs=[pl.BlockSpec((1, s, 2 * NSA_HD), lambda t, b, pr: (b, 0, t * n_pair + pr)),
                  pl.BlockSpec((1,) + w1p.shape[1:], lambda t, b, pr: (t, 0, 0, 0)),
                  pl.BlockSpec((1,) + posb.shape[1:], lambda t, b, pr: (t, 0, 0)),
                  pl.BlockSpec((1,) + w1.shape[1:], lambda t, b, pr: (t, 0, 0)),
                  pl.BlockSpec((1,) + w2.shape[1:], lambda t, b, pr: (t, 0, 0))],
        out_specs=pl.BlockSpec((1, 1, 2, n_chunk, NSA_HD), lambda t, b, pr: (t, b, pr, 0, 0)),
        compiler_params=_params(("parallel", "parallel", "parallel")),
        name="nsa_compress_prompt",
    )(kv, w1p, posb, w1, w2)


def _selection_mask_t(imp_t, q_pos, n_valid_rows, sc_ref):
    j = lax.broadcasted_iota(jnp.int32, imp_t.shape, 0)
    cur = lax.shift_right_arithmetic(q_pos, SEL_BLOCK.bit_length() - 1)
    valid = j <= cur
    forced = (j == 0) | (j == cur) | (j == cur - 1)
    score = jnp.where(valid, imp_t + jnp.where(forced, FORCE_BONUS, 0.0), NEG)
    sc_ref[...] = score

    def body(jp, cnt):
        row = sc_ref[pl.ds(jp, 1), :]
        beats = (row > score) | ((row == score) & (jp < j))
        return cnt + jnp.where(beats, 1.0, 0.0)

    cnt = lax.fori_loop(0, n_valid_rows, body, jnp.zeros(imp_t.shape, F32))
    return jnp.where((cnt < SEL_TOPK) & valid, 0.0, NEG)


def _nsa_prompt_kernel(q_ref, kc_ref, vc_ref, ks_ref, vs_ref, kw_ref, vw_ref, gt_ref, cov_ref,
                       bsel_ref, bwin_ref, o_ref, sc_ref, *, n_cmp):
    qi = pl.program_id(2)
    q0 = qi * Q_BLOCK
    rows = NSA_HPG * Q_BLOCK
    q4 = q_ref[0].reshape(rows, LANES)

    n_pad = kc_ref.shape[2]
    s_c = lax.dot_general(q4, kc_ref[0, 0], _NT, preferred_element_type=F32)
    q_pos = q0 + (lax.broadcasted_iota(jnp.int32, (rows, n_pad), 0) & (Q_BLOCK - 1))
    n = lax.broadcasted_iota(jnp.int32, (rows, n_pad), 1)
    mask_c = (n * CMP_STRIDE + CMP_BLOCK - 1 <= q_pos) & (n < n_cmp)
    s_c = jnp.where(mask_c, s_c, NEG)
    e = jnp.where(mask_c, jnp.exp(s_c - jnp.max(s_c, axis=-1, keepdims=True)), 0.0)
    p_c = e / jnp.maximum(jnp.sum(e, axis=-1, keepdims=True), 1e-30)
    o_c = jnp.dot(p_c.astype(BF16), vc_ref[0, 0], preferred_element_type=F32)
    p_sum = p_c[0:Q_BLOCK]
    for i in range(1, NSA_HPG):
        p_sum = p_sum + p_c[i * Q_BLOCK:(i + 1) * Q_BLOCK]
    imp_t = lax.dot_general(cov_ref[...], p_sum, _NT, precision=lax.Precision.HIGHEST,
                            preferred_element_type=F32)

    n_sel = imp_t.shape[0]
    qp_t = q0 + lax.broadcasted_iota(jnp.int32, (n_sel, Q_BLOCK), 1)
    msk_t = _selection_mask_t(imp_t, qp_t, (q0 + Q_BLOCK - 1) // SEL_BLOCK + 1, sc_ref)
    pad = [jnp.zeros((LANES - NSA_HD - n_sel, Q_BLOCK), F32)] if n_sel < LANES - NSA_HD else []
    msk = jnp.concatenate([jnp.zeros((NSA_HD, Q_BLOCK), F32), msk_t] + pad, axis=0).T
    q_aug = q4 + jnp.concatenate([msk.astype(BF16)] * NSA_HPG, axis=0)

    init = (jnp.full((rows, 1), NEG, F32), jnp.zeros((rows, 1), F32), jnp.zeros((rows, NSA_HD), F32))

    def sel_body(c, carry):
        k0 = pl.multiple_of(c * KEY_CHUNK, KEY_CHUNK)
        s = lax.dot_general(q_aug, ks_ref[0, 0, pl.ds(k0, KEY_CHUNK), :], _NT, preferred_element_type=F32)
        s = s + bsel_ref[jnp.clip(c - qi + 2, 0, 2)].reshape(rows, KEY_CHUNK)
        return _softmax_step(s, vs_ref[0, 0, pl.ds(k0, KEY_CHUNK), :], *carry)

    _, l_s, acc_s = lax.fori_loop(0, qi + 1, sel_body, init)
    o_s = acc_s / jnp.maximum(l_s, 1e-30)

    def win_body(c, carry):
        k0 = pl.multiple_of(c * KEY_CHUNK, KEY_CHUNK)
        s = lax.dot_general(q4, kw_ref[0, 0, pl.ds(k0, KEY_CHUNK), :], _NT, preferred_element_type=F32)
        s = s + bwin_ref[qi - c].reshape(rows, KEY_CHUNK)
        return _softmax_step(s, vw_ref[0, 0, pl.ds(k0, KEY_CHUNK), :], *carry)

    _, l_w, acc_w = lax.fori_loop(jnp.maximum(qi - WINDOW // KEY_CHUNK, 0), qi + 1, win_body, init)
    o_w = acc_w / jnp.maximum(l_w, 1e-30)

    gt = gt_ref[0, 0]
    for i in range(NSA_HPG):
        sl = slice(i * Q_BLOCK, (i + 1) * Q_BLOCK)
        o = (gt[:, i:i + 1] * o_c[sl] + gt[:, NSA_HPG + i:NSA_HPG + i + 1] * o_s[sl]
             + gt[:, 2 * NSA_HPG + i:2 * NSA_HPG + i + 1] * o_w[sl])
        o_ref[0, i] = o.astype(o_ref.dtype)


def nsa_prompt_attention(qh, kc, vc, ks, vs, kw, vw, gates, cover_t, bsel, bwin, n_cmp):
    bsz, _, s, _ = qh.shape
    nq = s // Q_BLOCK
    n_sel = cover_t.shape[0]
    per_bg = lambda b, g, i: (b, g, 0, 0)
    return pl.pallas_call(
        functools.partial(_nsa_prompt_kernel, n_cmp=n_cmp),
        out_shape=jax.ShapeDtypeStruct((bsz, NSA_HEADS, s, NSA_HD), BF16),
        grid=(bsz, NSA_GROUPS, nq),
        in_specs=[pl.BlockSpec((1, NSA_HPG, Q_BLOCK, LANES), lambda b, g, i: (b, g, i, 0)),
                  pl.BlockSpec((1, 1) + kc.shape[2:], per_bg),
                  pl.BlockSpec((1, 1) + vc.shape[2:], per_bg),
                  pl.BlockSpec((1, 1) + ks.shape[2:], per_bg),
                  pl.BlockSpec((1, 1) + vs.shape[2:], per_bg),
                  pl.BlockSpec((1, 1) + kw.shape[2:], per_bg),
                  pl.BlockSpec((1, 1) + vw.shape[2:], per_bg),
                  pl.BlockSpec((1, 1, Q_BLOCK, 3 * NSA_HPG), lambda b, g, i: (b, g, i, 0)),
                  pl.BlockSpec(cover_t.shape, lambda b, g, i: (0, 0)),
                  pl.BlockSpec((3, NSA_HPG, KEY_CHUNK, KEY_CHUNK), lambda b, g, i: (0, g, 0, 0)),
                  pl.BlockSpec((5, NSA_HPG, KEY_CHUNK, KEY_CHUNK), lambda b, g, i: (0, g, 0, 0))],
        out_specs=pl.BlockSpec((1, NSA_HPG, Q_BLOCK, NSA_HD), lambda b, g, i: (b, g, i, 0)),
        scratch_shapes=[pltpu.VMEM((n_sel, Q_BLOCK), F32)],
        compiler_params=_params(("parallel", "parallel", "arbitrary")),
        name="nsa_prompt_attention",
    )(qh, kc, vc, ks, vs, kw, vw, gates, cover_t, bsel, bwin)


def _cover_t(n_cmp, n_cmp_pad, n_sel, n_sel_pad):
    c_start = jnp.arange(n_cmp_pad)[None, :] * CMP_STRIDE
    s_start = jnp.arange(n_sel_pad)[:, None] * SEL_BLOCK
    cover = (c_start < s_start + SEL_BLOCK) & (c_start + CMP_BLOCK > s_start)
    cover = cover & (jnp.arange(n_cmp_pad)[None, :] < n_cmp) & (jnp.arange(n_sel_pad)[:, None] < n_sel)
    return cover.astype(F32)


def _nsa_weights(w_in, w_out, pos, w1, w2):
    scale = NSA_HD ** -0.5
    kv0 = NSA_Q_COLS
    g0 = NSA_Q_COLS + 6 * NSA_KV_COLS
    w_q = (w_in[:, :kv0] * scale).astype(BF16)
    w_kv = w_in[:, kv0:g0].astype(BF16)
    w_g = jnp.pad(w_in[:, g0:], ((0, 0), (0, LANES - 3 * NSA_HEADS))).astype(BF16)
    return w_q, w_kv, w_g, w_out.astype(BF16), _compress_weights(pos, w1, w2)


def nsa_prompt(x, wts, bsel, bwin):
    w_q, w_kv, w_g, _, cmp_w = wts
    bsz, s, d = x.shape
    x2 = x.reshape(bsz * s, d)
    (q,) = linear(x2, w_q, (BF16,))
    kv32, kv16 = linear(x2, w_kv, (F32, BF16))
    (gates,) = linear(x2, w_g, (F32,), act="sigmoid", tn=LANES)
    kv32 = kv32.reshape(bsz, s, 6 * NSA_KV_COLS)
    new_kv = kv32[..., :4 * NSA_KV_COLS].reshape(bsz, s, 4, NSA_GROUPS, NSA_HD)
    new_win = kv32[..., 4 * NSA_KV_COLS:].reshape(bsz, s, 2, NSA_GROUPS, NSA_HD)

    n_cmp = (s - CMP_BLOCK) // CMP_STRIDE + 1
    n_chunk = s // CMP_STRIDE
    n_sel = s // SEL_BLOCK
    cmp = compress_prompt(kv32, cmp_w)
    kc = jnp.pad(cmp[0], ((0, 0), (0, 0), (0, 0), (0, LANES - NSA_HD)))
    vc = cmp[1]

    qh = q.reshape(bsz, s, NSA_HEADS, NSA_HD).transpose(0, 2, 1, 3)
    qh = jnp.pad(qh, ((0, 0), (0, 0), (0, 0), (0, LANES - NSA_HD)))
    kvh = kv16.reshape(bsz, s, 6, NSA_GROUPS, NSA_HD).transpose(2, 0, 3, 1, 4)
    onehot = (jnp.arange(s)[:, None] // SEL_BLOCK == jnp.arange(LANES - NSA_HD)[None, :]).astype(BF16)
    ks = jnp.concatenate([kvh[2], jnp.broadcast_to(onehot, (bsz, NSA_GROUPS) + onehot.shape)], axis=-1)
    kw = jnp.pad(kvh[4], ((0, 0), (0, 0), (0, 0), (0, LANES - NSA_HD)))
    gates = gates[:, :3 * NSA_HEADS].reshape(bsz, s, 3, NSA_GROUPS, NSA_HPG)
    gates = gates.transpose(0, 3, 1, 2, 4).reshape(bsz, NSA_GROUPS, s, 3 * NSA_HPG)
    cover_t = _cover_t(n_cmp, n_chunk, n_sel, n_sel)
    o = nsa_prompt_attention(qh, kc, vc, ks, kvh[3], kw, kvh[5], gates, cover_t, bsel, bwin, n_cmp)
    o = o.transpose(0, 2, 1, 3).reshape(bsz * s, NSA_Q_COLS)
    keep = min(WINDOW, s)
    return o, new_kv, new_win[:, s - keep:]


def _lambda_full(lam_ref, lambda_init):
    lf = lam_ref[...]
    a = jnp.sum(lf[0:1] * lf[1:2], axis=-1, keepdims=True)
    b = jnp.sum(lf[2:3] * lf[3:4], axis=-1, keepdims=True)
    return jnp.exp(a) - jnp.exp(b) + lambda_init


def _sub_norm(o, g, lambda_init):
    return o * lax.rsqrt(jnp.mean(o * o, axis=-1, keepdims=True) + LN_EPS) * g * (1.0 - lambda_init)


def _diff_prompt_kernel(q_ref, k_ref, v_ref, bias_ref, lam_ref, g_ref, o_ref, *, lambda_init):
    qi = pl.program_id(2)
    q = q_ref[0]
    lane = lax.broadcasted_iota(jnp.int32, q.shape, 1)
    zero = jnp.zeros_like(q)
    q2 = jnp.concatenate([jnp.where(lane < DIFF_D, q, zero), jnp.where(lane >= DIFF_D, q, zero)], axis=0)
    rows = 2 * Q_BLOCK
    init = (jnp.full((rows, 1), NEG, F32), jnp.zeros((rows, 1), F32), jnp.zeros((rows, DIFF_VD), F32))

    def body(c, carry):
        k0 = pl.multiple_of(c * KEY_CHUNK, KEY_CHUNK)
        s = lax.dot_general(q2, k_ref[0, pl.ds(k0, KEY_CHUNK), :], _NT, preferred_element_type=F32)
        s = s + bias_ref[jnp.clip(c - qi + 2, 0, 2)].reshape(rows, KEY_CHUNK)
        return _softmax_step(s, v_ref[0, pl.ds(k0, KEY_CHUNK), :], *carry)

    _, l, acc = lax.fori_loop(0, qi + 1, body, init)
    o = acc / jnp.maximum(l, 1e-30)
    o = o[:Q_BLOCK] - _lambda_full(lam_ref, lambda_init) * o[Q_BLOCK:]
    o_ref[0] = _sub_norm(o, g_ref[...], lambda_init).astype(o_ref.dtype)


def diff_prompt_attention(q, kv, bsel, lam, g, lambda_init):
    bsz, s, _ = q.shape
    nq = s // Q_BLOCK
    return pl.pallas_call(
        functools.partial(_diff_prompt_kernel, lambda_init=lambda_init),
        out_shape=jax.ShapeDtypeStruct((bsz, s, DIFF_Q_COLS), BF16),
        grid=(bsz, DIFF_HEADS, nq),
        in_specs=[pl.BlockSpec((1, Q_BLOCK, DIFF_VD), lambda b, h, i: (b, i, h)),
                  pl.BlockSpec((1, s, DIFF_VD), lambda b, h, i: (b, 0, h)),
                  pl.BlockSpec((1, s, DIFF_VD), lambda b, h, i: (b, 0, DIFF_HEADS + h)),
                  pl.BlockSpec((3, 2, KEY_CHUNK, KEY_CHUNK), lambda b, h, i: (0, h, 0, 0)),
                  pl.BlockSpec(lam.shape, lambda b, h, i: (0, 0)),
                  pl.BlockSpec((1, DIFF_VD), lambda b, h, i: (0, 0))],
        out_specs=pl.BlockSpec((1, Q_BLOCK, DIFF_VD), lambda b, h, i: (b, i, h)),
        compiler_params=_params(("parallel", "parallel", "arbitrary")),
        name="diff_prompt_attention",
    )(q, kv, kv, bsel, lam, g.reshape(1, DIFF_VD))


def _diff_weights(w_in, w_out):
    scale = DIFF_D ** -0.5
    return (w_in[:, :DIFF_Q_COLS] * scale).astype(BF16), w_in[:, DIFF_Q_COLS:].astype(BF16), w_out.astype(BF16)


def diff_prompt(x, wts, lam, g, bsel, lambda_init):
    w_q, w_kv, _ = wts
    bsz, s, d = x.shape
    x2 = x.reshape(bsz * s, d)
    (q,) = linear(x2, w_q, (BF16,))
    kv32, kv16 = linear(x2, w_kv, (F32, BF16))
    o = diff_prompt_attention(q.reshape(bsz, s, -1), kv16.reshape(bsz, s, -1), bsel, lam, g, lambda_init)
    return o.reshape(bsz * s, DIFF_Q_COLS), kv32.reshape(bsz, s, 2, DIFF_HEADS, DIFF_VD)


def _diff_decode_kernel(pt_ref, qm_ref, k_ref, v_ref, bias_ref, kn_ref, vn_ref, b0_ref, lam_ref, g_ref,
                        o_ref, m_ref, l_ref, acc_ref, *, lambda_init):
    p = pl.program_id(1)

    @pl.when(p == 0)
    def _():
        m_ref[...] = jnp.full_like(m_ref, NEG)
        l_ref[...] = jnp.zeros_like(l_ref)
        acc_ref[...] = jnp.zeros_like(acc_ref)

    qm = qm_ref[0]
    s = lax.dot_general(qm, k_ref[...].astype(BF16), _NT, preferred_element_type=F32) + bias_ref[...]
    m, l, acc = _softmax_step(s, v_ref[...].astype(BF16), m_ref[...], l_ref[...], acc_ref[...])
    m_ref[...] = m
    l_ref[...] = l
    acc_ref[...] = acc

    @pl.when(p == pl.num_programs(1) - 1)
    def _():
        s_new = jnp.sum(qm.astype(F32) * kn_ref[0], axis=-1, keepdims=True) + b0_ref[...]
        m_new = jnp.maximum(m, s_new)
        a = jnp.exp(m - m_new)
        p_new = jnp.exp(s_new - m_new)
        l_f = a * l + p_new
        acc_f = a * acc + p_new * vn_ref[0]
        lam_full = _lambda_full(lam_ref, lambda_init)
        o_n = acc_f / jnp.maximum(l_f, 1e-30)
        for h in range(DIFF_HEADS):
            blk = o_n[2 * h:2 * h + 2, h * DIFF_VD:(h + 1) * DIFF_VD]
            o_h = blk[0:1] - lam_full * blk[1:2]
            o_ref[0, :, h * DIFF_VD:(h + 1) * DIFF_VD] = _sub_norm(o_h, g_ref[...], lambda_init)


def diff_decode_attention(page_table, qm, cache, li, key_bias, k_new, v_new, bias0, lam, g, lambda_init):
    bsz, n_pages = page_table.shape
    page = cache.shape[2]
    cols = DIFF_Q_COLS
    n_maps = 2 * DIFF_HEADS
    grid_spec = pltpu.PrefetchScalarGridSpec(
        num_scalar_prefetch=1,
        grid=(bsz, n_pages),
        in_specs=[pl.BlockSpec((1, n_maps, cols), lambda b, p, pt: (b, 0, 0)),
                  pl.BlockSpec((None, None, page, cols), lambda b, p, pt: (li, pt[b, p], 0, 0)),
                  pl.BlockSpec((None, None, page, cols), lambda b, p, pt: (li, pt[b, p], 0, 1)),
                  pl.BlockSpec((n_maps, page), lambda b, p, pt: (0, p)),
                  pl.BlockSpec((1, 1, cols), lambda b, p, pt: (b, 0, 0)),
                  pl.BlockSpec((1, 1, cols), lambda b, p, pt: (b, 0, 0)),
                  pl.BlockSpec((n_maps, 1), lambda b, p, pt: (0, 0)),
                  pl.BlockSpec(lam.shape, lambda b, p, pt: (0, 0)),
                  pl.BlockSpec((1, DIFF_VD), lambda b, p, pt: (0, 0))],
        out_specs=pl.BlockSpec((1, 1, cols), lambda b, p, pt: (b, 0, 0)),
        scratch_shapes=[pltpu.VMEM((n_maps, 1), F32), pltpu.VMEM((n_maps, 1), F32),
                        pltpu.VMEM((n_maps, cols), F32)],
    )
    return pl.pallas_call(
        functools.partial(_diff_decode_kernel, lambda_init=lambda_init),
        out_shape=jax.ShapeDtypeStruct((bsz, 1, cols), F32),
        grid_spec=grid_spec,
        compiler_params=_params(("parallel", "arbitrary")),
        name="diff_decode_attention",
    )(page_table, qm, cache, cache, key_bias, k_new, v_new, bias0, lam, g.reshape(1, DIFF_VD))


def diff_decode(x, wts, cache, li, page_table, rel_bias, lam, g, lambda_init):
    w_q, w_kv, _ = wts
    bsz = x.shape[0]
    n_pool, page = cache.shape[1], cache.shape[2]
    past_len = page_table.shape[1] * page
    x2 = x.reshape(bsz, D_MODEL)
    (q,) = linear(x2, w_q, (F32,))
    (kv,) = linear(x2, w_kv, (F32,))
    n_maps = 2 * DIFF_HEADS
    lane_map = jnp.arange(DIFF_Q_COLS)[None, :] // DIFF_D
    qm = jnp.where(lane_map == jnp.arange(n_maps)[:, None], q[:, None, :], 0.0).astype(BF16)
    key_bias = _key_bias(rel_bias, past_len, jnp.arange(past_len)).T
    bias0 = _key_bias(rel_bias, past_len, jnp.array([past_len])).T
    far = rel_bias[N_BUCKETS - 1][:, None]
    cache2 = cache.reshape(cache.shape[0], n_pool, page, 2 * DIFF_Q_COLS)
    o = diff_decode_attention(page_table, qm, cache2, li, key_bias - far, kv[:, None, :DIFF_Q_COLS],
                              kv[:, None, DIFF_Q_COLS:], bias0 - far, lam, g, lambda_init)
    return o.reshape(bsz, DIFF_Q_COLS), kv.reshape(bsz, 1, 2, DIFF_HEADS, DIFF_VD)


def _nsa_decode_kernel(pt_ref, cache_ref, qmat_ref, kvn_ref, gt_ref, win_ref, w1p_ref, posb_ref, w1_ref,
                       w2_ref, covt_ref, gsum_ref, kbias_ref, wbias_ref, b0_ref, o_ref,
                       bufc, bufs, sc_ref, msk_ref, sem, *, li, n_pages, page, past_len):
    b = pl.program_id(0)
    half = 2 * NSA_KV_COLS
    n_heads = NSA_HEADS

    n_slab = half // (2 * NSA_HD)

    def page_copies(p, part):
        pg = pt_ref[b, p]
        rows = pl.ds(p * page, page)
        if part == 1:
            return [pltpu.make_async_copy(cache_ref.at[li, pg, :, pl.ds(half, half)], bufs.at[rows, :],
                                          sem.at[1])]
        return [pltpu.make_async_copy(cache_ref.at[li, pg, :, pl.ds(c * 2 * NSA_HD, 2 * NSA_HD)],
                                      bufc.at[c, rows, :], sem.at[0]) for c in range(n_slab)]

    def start_all(p, _):
        for cp in page_copies(p, 0) + page_copies(p, 1):
            cp.start()
        return 0

    lax.fori_loop(0, n_pages, start_all, 0)

    def wait_part(part):
        def body(p, _):
            for cp in page_copies(p, part):
                cp.wait()
            return 0
        lax.fori_loop(0, n_pages, body, 0)

    qmat = qmat_ref[0]
    q_pos = past_len
    n_chunk = past_len // CMP_STRIDE
    n_cmp = (past_len + 1 - CMP_BLOCK) // CMP_STRIDE + 1
    assert n_cmp + CMP_BLOCK // CMP_STRIDE - 1 <= n_chunk

    wait_part(0)
    cmp = []
    for t in range(2):
        pos_term = _pos_term(posb_ref.at[t], w1_ref.at[t])
        outs = []
        for pair in range(NSA_GROUPS // 2):
            outs += _compress_pair(bufc.at[t * (NSA_GROUPS // 2) + pair], n_chunk, w1p_ref.at[t], pos_term,
                                   w2_ref[t])
        cmp.append(outs)
    s_c = jnp.zeros((n_chunk, n_heads), F32)
    for g in range(NSA_GROUPS):
        s_c = s_c + jnp.dot(cmp[0][g].astype(BF16), qmat[g * NSA_HD:(g + 1) * NSA_HD],
                            preferred_element_type=F32)
    n = lax.broadcasted_iota(jnp.int32, s_c.shape, 0)
    mask_c = (n * CMP_STRIDE + CMP_BLOCK - 1 <= q_pos) & (n < n_cmp)
    s_c = jnp.where(mask_c, s_c, NEG)
    e = jnp.where(mask_c, jnp.exp(s_c - jnp.max(s_c, axis=0, keepdims=True)), 0.0)
    p_c = e / jnp.maximum(jnp.sum(e, axis=0, keepdims=True), 1e-30)
    p_cb = p_c.astype(BF16)
    o_c = jnp.concatenate([lax.dot_general(cmp[1][g].astype(BF16), p_cb, _TN, preferred_element_type=F32)
                           for g in range(NSA_GROUPS)], axis=0)
    p_sum = jnp.dot(p_c, gsum_ref[...], precision=lax.Precision.HIGHEST, preferred_element_type=F32)
    imp_t = jnp.dot(covt_ref[...], p_sum, precision=lax.Precision.HIGHEST, preferred_element_type=F32)

    qp_t = jnp.full(imp_t.shape, q_pos, jnp.int32)
    msk_ref[...] = _selection_mask_t(imp_t, qp_t, q_pos // SEL_BLOCK + 1, sc_ref)

    wait_part(1)
    blocks_per_page = page // SEL_BLOCK
    sub = lax.broadcasted_iota(jnp.int32, (page, n_heads), 0)

    def sel_page(p, carry, bias):
        m, l, acc = carry
        r0 = p * page if isinstance(p, int) else pl.multiple_of(p * page, page)
        k = bufs[pl.ds(r0, page), 0:NSA_KV_COLS].astype(BF16)
        v = bufs[pl.ds(r0, page), NSA_KV_COLS:half].astype(BF16)
        s = jnp.dot(k, qmat, preferred_element_type=F32)
        if bias is not None:
            s = s + bias
        mrow = msk_ref[pl.ds(p * blocks_per_page, 1), :]
        for t in range(1, blocks_per_page):
            mrow = jnp.where(sub >= t * SEL_BLOCK, msk_ref[pl.ds(p * blocks_per_page + t, 1), :], mrow)
        s = s + mrow
        m_new = jnp.maximum(m, jnp.max(s, axis=0, keepdims=True))
        pr = jnp.where(s > HALF_NEG, jnp.exp(s - m_new), 0.0)
        a = jnp.exp(m - m_new)
        l = a * l + jnp.sum(pr, axis=0, keepdims=True)
        acc = a * acc + lax.dot_general(v, pr.astype(BF16), _TN, preferred_element_type=F32)
        return m_new, l, acc

    init = (jnp.full((1, n_heads), NEG, F32), jnp.zeros((1, n_heads), F32),
            jnp.zeros((NSA_KV_COLS, n_heads), F32))
    carry = lax.fori_loop(0, n_pages - 1, lambda p, c: sel_page(p, c, None), init)
    m, l, acc = sel_page(n_pages - 1, carry, kbias_ref[...])
    kvn = kvn_ref[0]
    qmat_f = qmat.astype(F32)

    def new_row(k_col, v_col, extra, m, l, acc):
        s_new = jnp.sum(k_col * qmat_f, axis=0, keepdims=True) + extra
        m_new = jnp.maximum(m, s_new)
        p_new = jnp.where(s_new > HALF_NEG, jnp.exp(s_new - m_new), 0.0)
        a = jnp.exp(m - m_new)
        return a * l + p_new, a * acc + v_col * p_new

    new_block = past_len // SEL_BLOCK
    l, acc = new_row(kvn[2 * NSA_KV_COLS:3 * NSA_KV_COLS], kvn[3 * NSA_KV_COLS:4 * NSA_KV_COLS],
                     b0_ref[...] + msk_ref[pl.ds(new_block, 1), :], m, l, acc)
    o_s = acc / jnp.maximum(l, 1e-30)

    win = win_ref[0]
    s_w = jnp.dot(win[:, 0:NSA_KV_COLS].astype(BF16), qmat, preferred_element_type=F32) + wbias_ref[...]
    m_w = jnp.max(s_w, axis=0, keepdims=True)
    p_w = jnp.where(s_w > HALF_NEG, jnp.exp(s_w - m_w), 0.0)
    l_w = jnp.sum(p_w, axis=0, keepdims=True)
    acc_w = lax.dot_general(win[:, NSA_KV_COLS:half].astype(BF16), p_w.astype(BF16), _TN,
                            preferred_element_type=F32)
    l_w, acc_w = new_row(kvn[4 * NSA_KV_COLS:5 * NSA_KV_COLS], kvn[5 * NSA_KV_COLS:6 * NSA_KV_COLS],
                         b0_ref[...], m_w, l_w, acc_w)
    o_w = acc_w / jnp.maximum(l_w, 1e-30)

    gt = gt_ref[0]
    o_ref[0] = gt[0:1] * o_c + gt[1:2] * o_s + gt[2:3] * o_w


def nsa_decode_attention(page_table, cache, li, qmat, kv_new, gates, win, cmp_w, cover_t, gsum, key_bias,
                         win_bias, bias0):
    bsz, n_pages = page_table.shape
    page = cache.shape[2]
    past_len = n_pages * page
    assert page >= MAX_DISTANCE and page % SEL_BLOCK == 0
    w1p, posb, w1, w2 = cmp_w
    n_selp = cover_t.shape[0]
    full = lambda a: pl.BlockSpec(a.shape, lambda b, pt, _n=a.ndim: (0,) * _n)
    per_b = lambda a: pl.BlockSpec((1,) + a.shape[1:], lambda b, pt, _n=a.ndim: (b,) + (0,) * (_n - 1))
    grid_spec = pltpu.PrefetchScalarGridSpec(
        num_scalar_prefetch=1,
        grid=(bsz,),
        in_specs=[pl.BlockSpec(memory_space=pl.ANY), per_b(qmat), per_b(kv_new), per_b(gates),
                  per_b(win), full(w1p), full(posb), full(w1), full(w2), full(cover_t), full(gsum),
                  full(key_bias), full(win_bias), full(bias0)],
        out_specs=pl.BlockSpec((1, NSA_KV_COLS, NSA_HEADS), lambda b, pt: (b, 0, 0)),
        scratch_shapes=[pltpu.VMEM((NSA_GROUPS, past_len, 2 * NSA_HD), F32),
                        pltpu.VMEM((past_len, 2 * NSA_KV_COLS), F32),
                        pltpu.VMEM((n_selp, NSA_HEADS), F32),
                        pltpu.VMEM((n_selp, NSA_HEADS), F32),
                        pltpu.SemaphoreType.DMA((2,))],
    )
    return pl.pallas_call(
        functools.partial(_nsa_decode_kernel, li=li, n_pages=n_pages, page=page, past_len=past_len),
        out_shape=jax.ShapeDtypeStruct((bsz, NSA_KV_COLS, NSA_HEADS), F32),
        grid_spec=grid_spec,
        compiler_params=_params(("arbitrary",)),
        name="nsa_decode_attention",
    )(page_table, cache, qmat, kv_new, gates, win, w1p, posb, w1, w2, cover_t, gsum, key_bias, win_bias,
      bias0)


def nsa_decode(x, wts, cache, li, win_state, page_table, rel_bias):
    w_q, w_kv, w_g, _, cmp_w = wts
    bsz = x.shape[0]
    n_pool, page = cache.shape[1], cache.shape[2]
    past_len = page_table.shape[1] * page
    wb = win_state.shape[1]
    x2 = x.reshape(bsz, D_MODEL)
    (q,) = linear(x2, w_q, (F32,))
    (kv,) = linear(x2, w_kv, (F32,))
    (gates,) = linear(x2, w_g, (F32,), act="sigmoid", tn=LANES)
    qh = q.reshape(bsz, NSA_GROUPS, NSA_HPG, NSA_HD)
    same_g = jnp.arange(NSA_GROUPS)[:, None] == jnp.arange(NSA_GROUPS)[None, :]
    qmat = jnp.where(same_g[None, :, None, :, None], jnp.transpose(qh, (0, 1, 3, 2))[:, :, :, None, :], 0.0)
    qmat = qmat.reshape(bsz, NSA_KV_COLS, NSA_HEADS).astype(BF16)
    gates = gates[:, :3 * NSA_HEADS].reshape(bsz, 3, NSA_HEADS)

    t = past_len + 1
    n_cmp = (t - CMP_BLOCK) // CMP_STRIDE + 1
    n_chunk = past_len // CMP_STRIDE
    n_sel = -(-t // SEL_BLOCK)
    n_selp = -(-n_sel // LANES) * LANES
    cover_t = _cover_t(n_cmp, n_chunk, n_sel, n_selp)
    head = jnp.arange(NSA_HEADS)
    gsum = (head[:, None] // NSA_HPG == head[None, :] // NSA_HPG).astype(F32)
    far = rel_bias[N_BUCKETS - 1][None, :]
    key_bias = _key_bias(rel_bias, past_len, past_len - page + jnp.arange(page)) - far
    dist_w = wb - jnp.arange(wb)
    win_bias = _key_bias(rel_bias, past_len, past_len - dist_w) - far
    win_bias = jnp.where((dist_w <= WINDOW)[:, None], win_bias, NEG)
    bias0 = _key_bias(rel_bias, past_len, jnp.array([past_len])) - far
    cache2 = cache.reshape(cache.shape[0], n_pool, page, 4 * NSA_KV_COLS)
    win2 = win_state.reshape(bsz, wb, 2 * NSA_KV_COLS)
    o_t = nsa_decode_attention(page_table, cache2, li, qmat, kv[:, :, None], gates, win2, cmp_w, cover_t,
                               gsum, key_bias, win_bias, bias0)
    o_t = o_t.reshape(bsz, NSA_GROUPS, NSA_HD, NSA_GROUPS, NSA_HPG)
    o = jnp.stack([o_t[:, g, :, g, :] for g in range(NSA_GROUPS)], axis=1)
    o = jnp.transpose(o, (0, 1, 3, 2)).reshape(bsz, NSA_Q_COLS)
    new_kv = kv[:, :4 * NSA_KV_COLS].reshape(bsz, 1, 4, NSA_GROUPS, NSA_HD)
    new_win = kv[:, 4 * NSA_KV_COLS:].reshape(bsz, 1, 2, NSA_GROUPS, NSA_HD)
    win_all = jnp.concatenate([win_state, new_win], axis=1)
    keep = min(WINDOW, wb + 1)
    return o, new_kv, win_all[:, wb + 1 - keep:]


def kernel(x_prompt, x_sample, cache_nsa_kv, state_nsa_win, cache_diff_kv, page_table, rel_bias,
           nsa_w_in, nsa_w_out, nsa_cmp_pos, nsa_cmp_w1, nsa_cmp_w2,
           diff_w_in, diff_w_out, diff_lambda, diff_subln_g,
           mlp_w_up, mlp_w_down, ln_g, ln_b):
    bp, sp, d = x_prompt.shape
    bs = x_sample.shape[0]
    xp = x_prompt.reshape(bp * sp, d)
    xs = x_sample.reshape(bs, d)
    bsel, bwin = _bias_tiles(rel_bias)
    nsa_kv_p, nsa_kv_s, nsa_win_p, nsa_win_s, diff_kv_p, diff_kv_s = [], [], [], [], [], []
    for i in range(DEPTH):
        li = i // N_MIXERS
        if i % N_MIXERS == 0:
            wts = _nsa_weights(nsa_w_in[li], nsa_w_out[li], nsa_cmp_pos[li], nsa_cmp_w1[li], nsa_cmp_w2[li])
            op, kvp, wp = nsa_prompt(xp.reshape(bp, sp, d), wts, bsel, bwin)
            os_, kvs, wsn = nsa_decode(xs.reshape(bs, 1, d), wts, cache_nsa_kv, li, state_nsa_win[li],
                                       page_table, rel_bias)
            nsa_kv_p.append(kvp)
            nsa_kv_s.append(kvs)
            nsa_win_p.append(wp)
            nsa_win_s.append(wsn)
            w_out = wts[3]
        else:
            lambda_init = 0.8 - 0.6 * math.exp(-0.3 * i)
            wts = _diff_weights(diff_w_in[li], diff_w_out[li])
            op, kvp = diff_prompt(xp.reshape(bp, sp, d), wts, diff_lambda[li], diff_subln_g[li], bsel,
                                  lambda_init)
            os_, kvs = diff_decode(xs.reshape(bs, 1, d), wts, cache_diff_kv, li, page_table, rel_bias,
                                   diff_lambda[li], diff_subln_g[li], lambda_init)
            diff_kv_p.append(kvp)
            diff_kv_s.append(kvs)
            w_out = wts[2]
        w_up, w_down = mlp_w_up[i].astype(BF16), mlp_w_down[i].astype(BF16)
        xp = linear_post_norm(op, w_out, xp, ln_g[i, 0], ln_b[i, 0])
        xs = linear_post_norm(os_, w_out, xs, ln_g[i, 0], ln_b[i, 0])
        xp = mlp_post_norm(xp, w_up, w_down, ln_g[i, 1], ln_b[i, 1])
        xs = mlp_post_norm(xs, w_up, w_down, ln_g[i, 1], ln_b[i, 1])
    return (xp.reshape(bp, sp, d), xs.reshape(bs, 1, d), jnp.stack(nsa_kv_p), jnp.stack(nsa_kv_s),
            jnp.stack(nsa_win_p), jnp.stack(nsa_win_s), jnp.stack(diff_kv_p), jnp.stack(diff_kv_s))
```

```python
import functools
import math

import jax
import jax.numpy as jnp
from jax import lax
from jax.experimental import pallas as pl
from jax.experimental.pallas import tpu as pltpu

F32 = jnp.float32
BF16 = jnp.bfloat16

D_MODEL = 1024
DEPTH = 4
N_MIXERS = 2
N_BUCKETS = 32
MAX_DISTANCE = 128
NSA_HEADS = 16
NSA_GROUPS = 4
NSA_HPG = NSA_HEADS // NSA_GROUPS
NSA_HD = D_MODEL // NSA_HEADS
CMP_BLOCK = 32
CMP_STRIDE = 16
CMP_HIDDEN = 2 * NSA_HD
SEL_BLOCK = 64
SEL_TOPK = 16
WINDOW = 512
FORCE_BONUS = 1e4
NSA_KV_COLS = NSA_GROUPS * NSA_HD
NSA_Q_COLS = NSA_HEADS * NSA_HD
DIFF_HEADS = 8
DIFF_D = D_MODEL // (2 * DIFF_HEADS)
DIFF_VD = 2 * DIFF_D
DIFF_Q_COLS = DIFF_HEADS * DIFF_VD
D_FF = 4 * D_MODEL
ALPHA = (2 * DEPTH) ** 0.25
LN_EPS = 1e-5
NEG = -1e30
HALF_NEG = -5e29

LOG2E = math.log2(math.e)
LANES = 128
NSA_TILE = 256
DIFF_TILE = 512
PAGES_PER_STEP = 4
VMEM_LIMIT = 56 * 1024 * 1024

_NT = (((1,), (1,)), ((), ()))
_TN = (((0,), (0,)), ((), ()))


def _params(sem, vmem=VMEM_LIMIT):
    return pltpu.CompilerParams(dimension_semantics=sem, vmem_limit_bytes=vmem)


def _linear_kernel(a_ref, w_ref, *o_refs, act):
    y = jnp.dot(a_ref[...].astype(BF16), w_ref[...], preferred_element_type=F32)
    if act == "sigmoid":
        y = jax.nn.sigmoid(y)
    for o_ref in o_refs:
        o_ref[...] = y.astype(o_ref.dtype)


def linear(a, w, out_dtypes, act=None, tm=512, tn=512):
    m, k = a.shape
    n = w.shape[1]
    tm, tn = min(tm, m), min(tn, n)
    assert m % tm == 0 and n % tn == 0
    outs = pl.pallas_call(
        functools.partial(_linear_kernel, act=act),
        out_shape=[jax.ShapeDtypeStruct((m, n), dt) for dt in out_dtypes],
        grid=(m // tm, n // tn),
        in_specs=[pl.BlockSpec((tm, k), lambda i, j: (i, 0)),
                  pl.BlockSpec((k, tn), lambda i, j: (0, j))],
        out_specs=[pl.BlockSpec((tm, tn), lambda i, j: (i, j)) for _ in out_dtypes],
        compiler_params=_params(("parallel", "parallel")),
        name="linear",
    )(a, w)
    return outs


def _post_norm(resid, sub, g, b):
    z = ALPHA * resid + sub
    mu = jnp.mean(z, axis=-1, keepdims=True)
    zc = z - mu
    var = jnp.mean(zc * zc, axis=-1, keepdims=True)
    return zc * lax.rsqrt(var + LN_EPS) * g + b


def _linear_ln_kernel(a_ref, w_ref, r_ref, g_ref, b_ref, o_ref):
    y = jnp.dot(a_ref[...].astype(BF16), w_ref[...], preferred_element_type=F32)
    o_ref[...] = _post_norm(r_ref[...], y, g_ref[...], b_ref[...])


def linear_post_norm(a, w, resid, g, b, tm=512):
    m, k = a.shape
    n = w.shape[1]
    tm = min(tm, m)
    assert m % tm == 0
    return pl.pallas_call(
        _linear_ln_kernel,
        out_shape=jax.ShapeDtypeStruct((m, n), F32),
        grid=(m // tm,),
        in_specs=[pl.BlockSpec((tm, k), lambda i: (i, 0)),
                  pl.BlockSpec((k, n), lambda i: (0, 0)),
                  pl.BlockSpec((tm, n), lambda i: (i, 0)),
                  pl.BlockSpec((1, n), lambda i: (0, 0)),
                  pl.BlockSpec((1, n), lambda i: (0, 0))],
        out_specs=pl.BlockSpec((tm, n), lambda i: (i, 0)),
        compiler_params=_params(("parallel",)),
        name="linear_post_norm",
    )(a, w, resid, g.reshape(1, n), b.reshape(1, n))


def _mlp_kernel(x_ref, wu_ref, wd_ref, g_ref, b_ref, o_ref, acc_ref):
    f = pl.program_id(1)

    @pl.when(f == 0)
    def _():
        acc_ref[...] = jnp.zeros_like(acc_ref)

    h = jnp.dot(x_ref[...].astype(BF16), wu_ref[...], preferred_element_type=F32)
    h = jnp.square(jnp.maximum(h, 0.0))
    acc_ref[...] += jnp.dot(h.astype(BF16), wd_ref[...], preferred_element_type=F32)

    @pl.when(f == pl.num_programs(1) - 1)
    def _():
        o_ref[...] = _post_norm(x_ref[...], acc_ref[...], g_ref[...], b_ref[...])


def mlp_post_norm(x, w_up, w_down, g, b, tm=512, tf=512):
    m, d = x.shape
    ff = w_up.shape[1]
    tm = min(tm, m)
    assert m % tm == 0 and ff % tf == 0
    return pl.pallas_call(
        _mlp_kernel,
        out_shape=jax.ShapeDtypeStruct((m, d), F32),
        grid=(m // tm, ff // tf),
        in_specs=[pl.BlockSpec((tm, d), lambda i, f: (i, 0)),
                  pl.BlockSpec((d, tf), lambda i, f: (0, f)),
                  pl.BlockSpec((tf, d), lambda i, f: (f, 0)),
                  pl.BlockSpec((1, d), lambda i, f: (0, 0)),
                  pl.BlockSpec((1, d), lambda i, f: (0, 0))],
        out_specs=pl.BlockSpec((tm, d), lambda i, f: (i, 0)),
        scratch_shapes=[pltpu.VMEM((tm, d), F32)],
        compiler_params=_params(("parallel", "arbitrary")),
        name="mlp_post_norm",
    )(x, w_up, w_down, g.reshape(1, d), b.reshape(1, d))


def _t5_bucket(dist):
    n = jnp.maximum(dist, 0)
    max_exact = N_BUCKETS // 2
    large = max_exact + (jnp.log(jnp.maximum(n, 1).astype(F32) / max_exact)
                         / math.log(MAX_DISTANCE / max_exact) * (N_BUCKETS - max_exact)).astype(jnp.int32)
    return jnp.where(n < max_exact, n, jnp.minimum(large, N_BUCKETS - 1))


def _bias_tiles(rel_bias, t):
    assert t >= MAX_DISTANCE
    r = jnp.arange(t)[:, None]
    c = jnp.arange(t)[None, :]
    far = rel_bias[N_BUCKETS - 1]

    def rel(off):
        return (jnp.moveaxis(rel_bias[_t5_bucket(off + r - c)], -1, 0) - far[:, None, None]) * LOG2E

    prev = rel(t)
    diag = jnp.where((c <= r)[None], rel(0), NEG)
    edge = jnp.where((c >= r)[None], jnp.zeros_like(prev), NEG)
    return prev, diag, edge


def _tiles_t(tiles, maps_per_step):
    x = jnp.stack(tiles)
    k, maps, t, _ = x.shape
    x = x.reshape(k, maps // maps_per_step, maps_per_step, t, t)
    return jnp.transpose(x, (0, 1, 4, 2, 3)).reshape(k, maps // maps_per_step, t, maps_per_step * t)


def _key_bias(rel_bias, q_pos, k_pos):
    return (rel_bias[_t5_bucket(q_pos - k_pos)] - rel_bias[N_BUCKETS - 1][None, :]) * LOG2E


def _softmax_step(s, v, m, l, acc):
    m_new = jnp.maximum(m, jnp.max(s, axis=0, keepdims=True))
    p = jnp.exp2(s - m_new)
    a = jnp.exp2(m - m_new)
    l = a * l + jnp.sum(p, axis=0, keepdims=True)
    acc = a * acc + lax.dot_general(v, p.astype(BF16), _TN, preferred_element_type=F32)
    return m_new, l, acc


def _compress_pair(x_ref, n_chunk, w1_ref, pos_term, w2):
    acc = jnp.zeros((n_chunk, 4 * CMP_HIDDEN), F32)
    for s in range(CMP_STRIDE):
        xs = x_ref[pl.ds(s, n_chunk, stride=CMP_STRIDE), :]
        acc = acc + jnp.dot(xs.astype(BF16), w1_ref[s], preferred_element_type=F32)
    outs = []
    for g2 in range(2):
        first = acc[:, g2 * 2 * CMP_HIDDEN: g2 * 2 * CMP_HIDDEN + CMP_HIDDEN]
        second = acc[:, g2 * 2 * CMP_HIDDEN + CMP_HIDDEN: (g2 + 1) * 2 * CMP_HIDDEN]
        second_next = jnp.concatenate([second[1:], second[:1]], axis=0)
        hid = first + second_next + pos_term
        outs.append(jnp.dot(jax.nn.gelu(hid).astype(BF16), w2, preferred_element_type=F32))
    return outs


def _compress_weights(pos, w1, w2):
    n_sub = CMP_BLOCK // CMP_STRIDE
    w1r = w1.reshape(2, n_sub, CMP_STRIDE, NSA_HD, CMP_HIDDEN)
    w1t = jnp.transpose(w1r, (0, 2, 3, 1, 4)).reshape(2, CMP_STRIDE, NSA_HD, n_sub * CMP_HIDDEN)
    zeros = jnp.zeros_like(w1t)
    w1p = jnp.concatenate([jnp.concatenate([w1t, zeros], -1), jnp.concatenate([zeros, w1t], -1)], axis=2)
    posb = jnp.broadcast_to(pos.reshape(2, 1, CMP_BLOCK * NSA_HD), (2, 8, CMP_BLOCK * NSA_HD))
    return w1p.astype(BF16), posb.astype(BF16), w1.astype(BF16), w2.astype(BF16)


def _pos_term(posb_ref, w1_ref):
    return jnp.dot(posb_ref[...], w1_ref[...], preferred_element_type=F32)[0:1]


def _compress_prompt_kernel(x_ref, w1p_ref, posb_ref, w1_ref, w2_ref, o_ref, *, n_chunk):
    pos_term = _pos_term(posb_ref.at[0], w1_ref.at[0])
    outs = _compress_pair(x_ref.at[0], n_chunk, w1p_ref.at[0], pos_term, w2_ref[0])
    for g2 in range(2):
        o_ref[0, 0, g2] = outs[g2].astype(o_ref.dtype)


def compress_prompt(kv, cmp_w):
    w1p, posb, w1, w2 = cmp_w
    bsz, s, _ = kv.shape
    n_chunk = s // CMP_STRIDE
    n_pair = NSA_GROUPS // 2
    return pl.pallas_call(
        functools.partial(_compress_prompt_kernel, n_chunk=n_chunk),
        out_shape=jax.ShapeDtypeStruct((2, bsz, NSA_GROUPS, n_chunk, NSA_HD), BF16),
        grid=(2, bsz, n_pair),
        in_specs=[pl.BlockSpec((1, s, 2 * NSA_HD), lambda t, b, pr: (b, 0, t * n_pair + pr)),
                  pl.BlockSpec((1,) + w1p.shape[1:], lambda t, b, pr: (t, 0, 0, 0)),
                  pl.BlockSpec((1,) + posb.shape[1:], lambda t, b, pr: (t, 0, 0)),
                  pl.BlockSpec((1,) + w1.shape[1:], lambda t, b, pr: (t, 0, 0)),
                  pl.BlockSpec((1,) + w2.shape[1:], lambda t, b, pr: (t, 0, 0))],
        out_specs=pl.BlockSpec((1, 1, 2, n_chunk, NSA_HD), lambda t, b, pr: (t, b, pr, 0, 0)),
        compiler_params=_params(("parallel", "parallel", "parallel")),
        name="nsa_compress_prompt",
    )(kv, w1p, posb, w1, w2)


def _selection_mask_t(imp_t, q_pos, n_valid_rows, sc_ref):
    j = lax.broadcasted_iota(jnp.int32, imp_t.shape, 0)
    cur = lax.shift_right_arithmetic(q_pos, SEL_BLOCK.bit_length() - 1)
    valid = j <= cur
    forced = (j == 0) | (j == cur) | (j == cur - 1)
    score = jnp.where(valid, imp_t + jnp.where(forced, FORCE_BONUS, 0.0), NEG)
    sc_ref[...] = score

    def body(jp, cnt):
        row = sc_ref[pl.ds(jp, 1), :]
        beats = (row > score) | ((row == score) & (jp < j))
        return cnt + jnp.where(beats, 1.0, 0.0)

    cnt = lax.fori_loop(0, n_valid_rows, body, jnp.zeros(imp_t.shape, F32))
    return jnp.where((cnt < SEL_TOPK) & valid, 0.0, NEG)


def _nsa_prompt_kernel(q_ref, kc_ref, vc_ref, ks_ref, vs_ref, kw_ref, vw_ref, gt_ref, cov_ref, bias_ref,
                       o_ref, sc_ref, *, n_cmp):
    t = NSA_TILE
    qi = pl.program_id(2)
    q0 = qi * t
    rows = NSA_HPG * t
    q4 = q_ref[0].reshape(rows, LANES)

    n_pad = kc_ref.shape[2]
    s_c = lax.dot_general(kc_ref[0, 0], q4, _NT, preferred_element_type=F32)
    q_pos = q0 + (lax.broadcasted_iota(jnp.int32, (n_pad, rows), 1) & (t - 1))
    n = lax.broadcasted_iota(jnp.int32, (n_pad, rows), 0)
    mask_c = (n * CMP_STRIDE + CMP_BLOCK - 1 <= q_pos) & (n < n_cmp)
    s_c = jnp.where(mask_c, s_c, NEG)
    e = jnp.where(mask_c, jnp.exp2(s_c - jnp.max(s_c, axis=0, keepdims=True)), 0.0)
    p_c = e / jnp.maximum(jnp.sum(e, axis=0, keepdims=True), 1e-30)
    o_c = lax.dot_general(vc_ref[0, 0], p_c.astype(BF16), _TN, preferred_element_type=F32)
    p_sum = p_c[:, 0:t]
    for i in range(1, NSA_HPG):
        p_sum = p_sum + p_c[:, i * t:(i + 1) * t]
    imp_t = jnp.dot(cov_ref[...], p_sum, precision=lax.Precision.HIGHEST,
                    preferred_element_type=F32)

    n_sel = imp_t.shape[0]
    qp_t = q0 + lax.broadcasted_iota(jnp.int32, (n_sel, t), 1)
    msk_t = _selection_mask_t(imp_t, qp_t, (q0 + t - 1) // SEL_BLOCK + 1, sc_ref)
    pad = [jnp.zeros((LANES - NSA_HD - n_sel, t), F32)] if n_sel < LANES - NSA_HD else []
    msk = jnp.concatenate([jnp.zeros((NSA_HD, t), F32), msk_t] + pad, axis=0).T
    q_aug = q4 + jnp.concatenate([msk.astype(BF16)] * NSA_HPG, axis=0)

    init = (jnp.full((1, rows), NEG, F32), jnp.zeros((1, rows), F32), jnp.zeros((NSA_HD, rows), F32))
    prev_tile, diag_tile, edge_tile = 0, 1, 2

    def step(q, k_ref, v_ref, chunk, width, tile, carry):
        k0 = pl.multiple_of(chunk * t, t)
        s = lax.dot_general(k_ref[0, 0, pl.ds(k0, width), :], q, _NT, preferred_element_type=F32)
        if tile is not None:
            s = s + bias_ref[tile, 0]
        return _softmax_step(s, v_ref[0, 0, pl.ds(k0, width), :], *carry)

    def maybe(pred, fn, carry):
        return lax.cond(pred, fn, lambda cr: cr, carry)

    n_far = jnp.maximum(qi - 1, 0)
    carry = lax.fori_loop(0, n_far // 2, lambda c, cr: step(q_aug, ks_ref, vs_ref, 2 * c, 2 * t, None, cr), init)
    carry = maybe(n_far % 2 == 1, lambda cr: step(q_aug, ks_ref, vs_ref, n_far - 1, t, None, cr), carry)
    carry = maybe(qi >= 1, lambda cr: step(q_aug, ks_ref, vs_ref, qi - 1, t, prev_tile, cr), carry)
    _, l_s, acc_s = step(q_aug, ks_ref, vs_ref, qi, t, diag_tile, carry)

    carry = maybe(qi >= 2, lambda cr: step(q4, kw_ref, vw_ref, qi - 2, t, edge_tile, cr), init)
    carry = maybe(qi >= 1, lambda cr: step(q4, kw_ref, vw_ref, qi - 1, t, prev_tile, cr), carry)
    _, l_w, acc_w = step(q4, kw_ref, vw_ref, qi, t, diag_tile, carry)

    gt = gt_ref[0, 0, 0]
    o = (gt[0:1] * o_c + (gt[1:2] / jnp.maximum(l_s, 1e-30)) * acc_s
         + (gt[2:3] / jnp.maximum(l_w, 1e-30)) * acc_w)
    halves = [jnp.concatenate([o[:, (2 * k) * t:(2 * k + 1) * t], o[:, (2 * k + 1) * t:(2 * k + 2) * t]],
                              axis=0).T for k in range(NSA_HPG // 2)]
    o_ref[0] = jnp.concatenate(halves, axis=1).astype(o_ref.dtype)


def nsa_prompt_attention(qh, kc, vc, ks, vs, kw, vw, gates, cover_t, bias, n_cmp):
    bsz, _, s, _ = qh.shape
    t = NSA_TILE
    assert s % (2 * t) == 0 and WINDOW == 2 * t and NSA_HPG % 2 == 0
    n_sel = cover_t.shape[0]
    rows = NSA_HPG * t
    per_bg = lambda b, g, i: (b, g, 0, 0)
    return pl.pallas_call(
        functools.partial(_nsa_prompt_kernel, n_cmp=n_cmp),
        out_shape=jax.ShapeDtypeStruct((bsz, s, NSA_Q_COLS), BF16),
        grid=(bsz, NSA_GROUPS, s // t),
        in_specs=[pl.BlockSpec((1, NSA_HPG, t, LANES), lambda b, g, i: (b, g, i, 0)),
                  pl.BlockSpec((1, 1) + kc.shape[2:], per_bg),
                  pl.BlockSpec((1, 1) + vc.shape[2:], per_bg),
                  pl.BlockSpec((1, 1) + ks.shape[2:], per_bg),
                  pl.BlockSpec((1, 1) + vs.shape[2:], per_bg),
                  pl.BlockSpec((1, 1) + kw.shape[2:], per_bg),
                  pl.BlockSpec((1, 1) + vw.shape[2:], per_bg),
                  pl.BlockSpec((1, 1, 1, 3, rows), lambda b, g, i: (b, g, i, 0, 0)),
                  pl.BlockSpec(cover_t.shape, lambda b, g, i: (0, 0)),
                  pl.BlockSpec((3, 1, t, rows), lambda b, g, i: (0, g, 0, 0))],
        out_specs=pl.BlockSpec((1, t, NSA_HPG * NSA_HD), lambda b, g, i: (b, i, g)),
        scratch_shapes=[pltpu.VMEM((n_sel, t), F32)],
        compiler_params=_params(("parallel", "parallel", "arbitrary")),
        name="nsa_prompt_attention",
    )(qh, kc, vc, ks, vs, kw, vw, gates, cover_t, bias)


def _cover_t(n_cmp, n_cmp_pad, n_sel, n_sel_pad):
    c_start = jnp.arange(n_cmp_pad)[None, :] * CMP_STRIDE
    s_start = jnp.arange(n_sel_pad)[:, None] * SEL_BLOCK
    cover = (c_start < s_start + SEL_BLOCK) & (c_start + CMP_BLOCK > s_start)
    cover = cover & (jnp.arange(n_cmp_pad)[None, :] < n_cmp) & (jnp.arange(n_sel_pad)[:, None] < n_sel)
    return cover.astype(F32)


def _nsa_weights(w_in, w_out, pos, w1, w2):
    scale = NSA_HD ** -0.5 * LOG2E
    kv0 = NSA_Q_COLS
    g0 = NSA_Q_COLS + 6 * NSA_KV_COLS
    w_q = (w_in[:, :kv0] * scale).astype(BF16)
    w_kv = w_in[:, kv0:g0].astype(BF16)
    w_g = jnp.pad(w_in[:, g0:], ((0, 0), (0, LANES - 3 * NSA_HEADS))).astype(BF16)
    return w_q, w_kv, w_g, w_out.astype(BF16), _compress_weights(pos, w1, w2)


def nsa_prompt(x, wts, bias):
    w_q, w_kv, w_g, _, cmp_w = wts
    bsz, s, d = x.shape
    x2 = x.reshape(bsz * s, d)
    (q,) = linear(x2, w_q, (BF16,))
    kv32, kv16 = linear(x2, w_kv, (F32, BF16))
    (gates,) = linear(x2, w_g, (F32,), act="sigmoid", tn=LANES)
    kv32 = kv32.reshape(bsz, s, 6 * NSA_KV_COLS)
    new_kv = kv32[..., :4 * NSA_KV_COLS].reshape(bsz, s, 4, NSA_GROUPS, NSA_HD)
    new_win = kv32[..., 4 * NSA_KV_COLS:].reshape(bsz, s, 2, NSA_GROUPS, NSA_HD)

    n_cmp = (s - CMP_BLOCK) // CMP_STRIDE + 1
    n_chunk = s // CMP_STRIDE
    n_sel = s // SEL_BLOCK
    cmp = compress_prompt(kv32, cmp_w)
    kc = jnp.pad(cmp[0], ((0, 0), (0, 0), (0, 0), (0, LANES - NSA_HD)))
    vc = cmp[1]

    qh = q.reshape(bsz, s, NSA_HEADS, NSA_HD).transpose(0, 2, 1, 3)
    qh = jnp.pad(qh, ((0, 0), (0, 0), (0, 0), (0, LANES - NSA_HD)))
    kvh = kv16.reshape(bsz, s, 6, NSA_GROUPS, NSA_HD).transpose(2, 0, 3, 1, 4)
    onehot = (jnp.arange(s)[:, None] // SEL_BLOCK == jnp.arange(LANES - NSA_HD)[None, :]).astype(BF16)
    ks = jnp.concatenate([kvh[2], jnp.broadcast_to(onehot, (bsz, NSA_GROUPS) + onehot.shape)], axis=-1)
    kw = jnp.pad(kvh[4], ((0, 0), (0, 0), (0, 0), (0, LANES - NSA_HD)))
    nq = s // NSA_TILE
    gates = gates[:, :3 * NSA_HEADS].reshape(bsz, nq, NSA_TILE, 3, NSA_GROUPS, NSA_HPG)
    gates = gates.transpose(0, 4, 1, 3, 5, 2).reshape(bsz, NSA_GROUPS, nq, 3, NSA_HPG * NSA_TILE)
    cover_t = _cover_t(n_cmp, n_chunk, n_sel, n_sel)
    o = nsa_prompt_attention(qh, kc, vc, ks, kvh[3], kw, kvh[5], gates, cover_t, bias, n_cmp)
    o = o.reshape(bsz * s, NSA_Q_COLS)
    keep = min(WINDOW, s)
    return o, new_kv, new_win[:, s - keep:]


def _lambda_full(lam_ref, lambda_init):
    lf = lam_ref[...]
    a = jnp.sum(lf[0:1] * lf[1:2], axis=-1, keepdims=True)
    b = jnp.sum(lf[2:3] * lf[3:4], axis=-1, keepdims=True)
    return jnp.exp(a) - jnp.exp(b) + lambda_init


def _sub_norm(o, g, lambda_init):
    return o * lax.rsqrt(jnp.mean(o * o, axis=-1, keepdims=True) + LN_EPS) * g * (1.0 - lambda_init)


def _diff_prompt_kernel(q_ref, k_ref, v_ref, bias_ref, lam_ref, g_ref, o_ref, *, lambda_init):
    t = DIFF_TILE
    qi = pl.program_id(2)
    q = q_ref[0]
    lane = lax.broadcasted_iota(jnp.int32, q.shape, 1)
    zero = jnp.zeros_like(q)
    q2 = jnp.concatenate([jnp.where(lane < DIFF_D, q, zero), jnp.where(lane >= DIFF_D, q, zero)], axis=0)
    rows = 2 * t
    init = (jnp.full((1, rows), NEG, F32), jnp.zeros((1, rows), F32), jnp.zeros((DIFF_VD, rows), F32))

    def step(chunk, tile, carry):
        k0 = pl.multiple_of(chunk * t, t)
        s = lax.dot_general(k_ref[0, pl.ds(k0, t), :], q2, _NT, preferred_element_type=F32)
        if tile is not None:
            s = s + bias_ref[tile, 0]
        return _softmax_step(s, v_ref[0, pl.ds(k0, t), :], *carry)

    carry = lax.fori_loop(0, jnp.maximum(qi - 1, 0), lambda c, cr: step(c, None, cr), init)
    carry = lax.cond(qi >= 1, lambda cr: step(qi - 1, 0, cr), lambda cr: cr, carry)
    _, l, acc = step(qi, 1, carry)
    o = acc / jnp.maximum(l, 1e-30)
    o = o[:, :t] - _lambda_full(lam_ref, lambda_init) * o[:, t:]
    o = o * lax.rsqrt(jnp.mean(o * o, axis=0, keepdims=True) + LN_EPS) * g_ref[...] * (1.0 - lambda_init)
    o_ref[0] = o.T.astype(o_ref.dtype)


def diff_prompt_attention(q, kv, bias, lam, g, lambda_init):
    bsz, s, _ = q.shape
    t = DIFF_TILE
    assert s % t == 0
    return pl.pallas_call(
        functools.partial(_diff_prompt_kernel, lambda_init=lambda_init),
        out_shape=jax.ShapeDtypeStruct((bsz, s, DIFF_Q_COLS), BF16),
        grid=(bsz, DIFF_HEADS, s // t),
        in_specs=[pl.BlockSpec((1, t, DIFF_VD), lambda b, h, i: (b, i, h)),
                  pl.BlockSpec((1, s, DIFF_VD), lambda b, h, i: (b, 0, h)),
                  pl.BlockSpec((1, s, DIFF_VD), lambda b, h, i: (b, 0, DIFF_HEADS + h)),
                  pl.BlockSpec((2, 1, t, 2 * t), lambda b, h, i: (0, h, 0, 0)),
                  pl.BlockSpec(lam.shape, lambda b, h, i: (0, 0)),
                  pl.BlockSpec((DIFF_VD, 1), lambda b, h, i: (0, 0))],
        out_specs=pl.BlockSpec((1, t, DIFF_VD), lambda b, h, i: (b, i, h)),
        compiler_params=_params(("parallel", "parallel", "arbitrary")),
        name="diff_prompt_attention",
    )(q, kv, kv, bias, lam, g.reshape(DIFF_VD, 1))


def _diff_weights(w_in, w_out):
    scale = DIFF_D ** -0.5 * LOG2E
    return (w_in[:, :DIFF_Q_COLS] * scale).astype(BF16), w_in[:, DIFF_Q_COLS:].astype(BF16), w_out.astype(BF16)


def diff_prompt(x, wts, lam, g, bias, lambda_init):
    w_q, w_kv, _ = wts
    bsz, s, d = x.shape
    x2 = x.reshape(bsz * s, d)
    (q,) = linear(x2, w_q, (BF16,))
    kv32, kv16 = linear(x2, w_kv, (F32, BF16))
    o = diff_prompt_attention(q.reshape(bsz, s, -1), kv16.reshape(bsz, s, -1), bias, lam, g, lambda_init)
    return o.reshape(bsz * s, DIFF_Q_COLS), kv32.reshape(bsz, s, 2, DIFF_HEADS, DIFF_VD)


def _diff_decode_kernel(pt_ref, q_ref, *refs, lambda_init, n_pg):
    c_refs = refs[:n_pg]
    bias_ref, kn_ref, vn_ref, b0_ref, lam_ref, g_ref, o_ref, m_ref, l_ref, acc_ref = refs[n_pg:]
    p = pl.program_id(1)
    last = pl.num_programs(1) - 1

    @pl.when(p == 0)
    def _():
        m_ref[...] = jnp.full_like(m_ref, NEG)
        l_ref[...] = jnp.zeros_like(l_ref)
        acc_ref[...] = jnp.zeros_like(acc_ref)

    q2 = q_ref[0]
    ks = [c_ref[:, 0].reshape(-1, DIFF_VD).astype(BF16) for c_ref in c_refs]
    vs = [c_ref[:, 1].reshape(-1, DIFF_VD).astype(BF16) for c_ref in c_refs]
    s = jnp.concatenate([lax.dot_general(q2, k, _NT, preferred_element_type=F32) for k in ks], axis=1)
    s = s + bias_ref[jnp.where(p == last, 1, 0)]
    m, l, acc = m_ref[...], l_ref[...], acc_ref[...]
    m_new = jnp.maximum(m, jnp.max(s, axis=-1, keepdims=True))
    pr = jnp.exp2(s - m_new)
    a = jnp.exp2(m - m_new)
    m, l = m_new, a * l + jnp.sum(pr, axis=-1, keepdims=True)
    acc = a * acc + jnp.dot(pr.astype(BF16), jnp.concatenate(vs, axis=0), preferred_element_type=F32)
    m_ref[...] = m
    l_ref[...] = l
    acc_ref[...] = acc

    @pl.when(p == last)
    def _():
        s_new = jnp.sum(q2.astype(F32) * kn_ref[0], axis=-1, keepdims=True) + b0_ref[...]
        m_new = jnp.maximum(m, s_new)
        a = jnp.exp2(m - m_new)
        p_new = jnp.exp2(s_new - m_new)
        o_n = (a * acc + p_new * vn_ref[0]) / jnp.maximum(a * l + p_new, 1e-30)
        lam_full = _lambda_full(lam_ref, lambda_init)
        for h in range(DIFF_HEADS):
            o_h = o_n[2 * h:2 * h + 1] - lam_full * o_n[2 * h + 1:2 * h + 2]
            o_ref[0, h:h + 1, :] = _sub_norm(o_h, g_ref[...], lambda_init)


def diff_decode_attention(page_table, q2, cache, li, bias, k_new, v_new, bias0, lam, g, lambda_init):
    bsz, n_pages = page_table.shape
    page = cache.shape[2]
    n_pg = PAGES_PER_STEP
    assert n_pages % n_pg == 0
    n_maps = 2 * DIFF_HEADS
    per_b = lambda b, p, pt: (b, 0, 0)

    def page_spec(k):
        return pl.BlockSpec((None, None, page, 2, DIFF_HEADS, DIFF_VD),
                            lambda b, p, pt: (li, pt[b, p * n_pg + k], 0, 0, 0, 0))

    grid_spec = pltpu.PrefetchScalarGridSpec(
        num_scalar_prefetch=1,
        grid=(bsz, n_pages // n_pg),
        in_specs=[pl.BlockSpec((1, n_maps, DIFF_VD), per_b)] + [page_spec(k) for k in range(n_pg)]
        + [pl.BlockSpec(bias.shape, lambda b, p, pt: (0, 0, 0)),
           pl.BlockSpec((1, n_maps, DIFF_VD), per_b),
           pl.BlockSpec((1, n_maps, DIFF_VD), per_b),
           pl.BlockSpec((n_maps, 1), lambda b, p, pt: (0, 0)),
           pl.BlockSpec(lam.shape, lambda b, p, pt: (0, 0)),
           pl.BlockSpec((1, DIFF_VD), lambda b, p, pt: (0, 0))],
        out_specs=pl.BlockSpec((1, DIFF_HEADS, DIFF_VD), per_b),
        scratch_shapes=[pltpu.VMEM((n_maps, 1), F32), pltpu.VMEM((n_maps, 1), F32),
                        pltpu.VMEM((n_maps, DIFF_VD), F32)],
    )
    return pl.pallas_call(
        functools.partial(_diff_decode_kernel, lambda_init=lambda_init, n_pg=n_pg),
        out_shape=jax.ShapeDtypeStruct((bsz, DIFF_HEADS, DIFF_VD), F32),
        grid_spec=grid_spec,
        compiler_params=_params(("parallel", "arbitrary")),
        name="diff_decode_attention",
    )(page_table, q2, *([cache] * n_pg), bias, k_new, v_new, bias0, lam, g.reshape(1, DIFF_VD))


def diff_decode(x, wts, cache, li, page_table, rel_bias, lam, g, lambda_init):
    w_q, w_kv, _ = wts
    bsz = x.shape[0]
    page = cache.shape[2]
    past_len = page_table.shape[1] * page
    x2 = x.reshape(bsz, D_MODEL)
    (q,) = linear(x2, w_q, (F32,))
    (kv,) = linear(x2, w_kv, (F32,))
    n_maps = 2 * DIFF_HEADS
    per_map = lambda a: jnp.repeat(a.reshape(bsz, DIFF_HEADS, DIFF_VD), 2, axis=1)
    own_lanes = jnp.arange(DIFF_VD)[None, :] // DIFF_D == jnp.arange(n_maps)[:, None] % 2
    q2 = jnp.where(own_lanes[None], per_map(q), 0.0).astype(BF16)
    n_keys = PAGES_PER_STEP * page
    assert n_keys >= MAX_DISTANCE
    col_head = jnp.arange(n_keys * DIFF_HEADS)[None, :] % DIFF_HEADS
    head_mask = jnp.where(col_head == jnp.arange(n_maps)[:, None] // 2, 0.0, NEG)
    near = _key_bias(rel_bias, past_len, past_len - n_keys + jnp.arange(n_keys))
    bias = jnp.stack([head_mask, head_mask + jnp.repeat(near.T, DIFF_HEADS, axis=1)])
    bias0 = _key_bias(rel_bias, past_len, jnp.array([past_len])).T
    o = diff_decode_attention(page_table, q2, cache, li, bias, per_map(kv[:, :DIFF_Q_COLS]),
                              per_map(kv[:, DIFF_Q_COLS:]), bias0, lam, g, lambda_init)
    return o.reshape(bsz, DIFF_Q_COLS), kv.reshape(bsz, 1, 2, DIFF_HEADS, DIFF_VD)


def _nsa_decode_kernel(pt_ref, cache_ref, qmat_ref, kvn_ref, gt_ref, win_ref, w1p_ref, posb_ref, w1_ref,
                       w2_ref, covt_ref, gsum_ref, kbias_ref, wbias_ref, b0_ref, o_ref,
                       bufc, bufs, sc_ref, msk_ref, sem, *, li, n_pages, page, past_len):
    b = pl.program_id(0)
    half = 2 * NSA_KV_COLS
    n_heads = NSA_HEADS
    n_slab = half // (2 * NSA_HD)

    def page_copies(p, part):
        pg = pt_ref[b, p]
        rows = pl.ds(p * page, page)
        if part == 1:
            return [pltpu.make_async_copy(cache_ref.at[li, pg, :, pl.ds(half, half)], bufs.at[rows, :],
                                          sem.at[1])]
        return [pltpu.make_async_copy(cache_ref.at[li, pg, :, pl.ds(c * 2 * NSA_HD, 2 * NSA_HD)],
                                      bufc.at[c, rows, :], sem.at[0]) for c in range(n_slab)]

    def start_all(p, _):
        for cp in page_copies(p, 0) + page_copies(p, 1):
            cp.start()
        return 0

    lax.fori_loop(0, n_pages, start_all, 0)

    def wait_part(part):
        def body(p, _):
            for cp in page_copies(p, part):
                cp.wait()
            return 0
        lax.fori_loop(0, n_pages, body, 0)

    qmat = qmat_ref[0]
    q_pos = past_len
    n_chunk = past_len // CMP_STRIDE
    n_cmp = (past_len + 1 - CMP_BLOCK) // CMP_STRIDE + 1
    assert n_cmp + CMP_BLOCK // CMP_STRIDE - 1 <= n_chunk

    wait_part(0)
    cmp = []
    for t in range(2):
        pos_term = _pos_term(posb_ref.at[t], w1_ref.at[t])
        outs = []
        for pair in range(NSA_GROUPS // 2):
            outs += _compress_pair(bufc.at[t * (NSA_GROUPS // 2) + pair], n_chunk, w1p_ref.at[t], pos_term,
                                   w2_ref[t])
        cmp.append(outs)
    s_c = jnp.zeros((n_chunk, n_heads), F32)
    for g in range(NSA_GROUPS):
        s_c = s_c + jnp.dot(cmp[0][g].astype(BF16), qmat[g * NSA_HD:(g + 1) * NSA_HD],
                            preferred_element_type=F32)
    n = lax.broadcasted_iota(jnp.int32, s_c.shape, 0)
    mask_c = (n * CMP_STRIDE + CMP_BLOCK - 1 <= q_pos) & (n < n_cmp)
    s_c = jnp.where(mask_c, s_c, NEG)
    e = jnp.where(mask_c, jnp.exp2(s_c - jnp.max(s_c, axis=0, keepdims=True)), 0.0)
    p_c = e / jnp.maximum(jnp.sum(e, axis=0, keepdims=True), 1e-30)
    p_cb = p_c.astype(BF16)
    o_c = jnp.concatenate([lax.dot_general(cmp[1][g].astype(BF16), p_cb, _TN, preferred_element_type=F32)
                           for g in range(NSA_GROUPS)], axis=0)
    p_sum = jnp.dot(p_c, gsum_ref[...], precision=lax.Precision.HIGHEST, preferred_element_type=F32)
    imp_t = jnp.dot(covt_ref[...], p_sum, precision=lax.Precision.HIGHEST, preferred_element_type=F32)

    qp_t = jnp.full(imp_t.shape, q_pos, jnp.int32)
    msk_ref[...] = _selection_mask_t(imp_t, qp_t, q_pos // SEL_BLOCK + 1, sc_ref)

    wait_part(1)
    blocks_per_page = page // SEL_BLOCK
    sub = lax.broadcasted_iota(jnp.int32, (page, n_heads), 0)

    def sel_page(p, carry, bias):
        m, l, acc = carry
        r0 = p * page if isinstance(p, int) else pl.multiple_of(p * page, page)
        k = bufs[pl.ds(r0, page), 0:NSA_KV_COLS].astype(BF16)
        v = bufs[pl.ds(r0, page), NSA_KV_COLS:half].astype(BF16)
        s = jnp.dot(k, qmat, preferred_element_type=F32)
        if bias is not None:
            s = s + bias
        mrow = msk_ref[pl.ds(p * blocks_per_page, 1), :]
        for t in range(1, blocks_per_page):
            mrow = jnp.where(sub >= t * SEL_BLOCK, msk_ref[pl.ds(p * blocks_per_page + t, 1), :], mrow)
        s = s + mrow
        m_new = jnp.maximum(m, jnp.max(s, axis=0, keepdims=True))
        pr = jnp.where(s > HALF_NEG, jnp.exp2(s - m_new), 0.0)
        a = jnp.exp2(m - m_new)
        l = a * l + jnp.sum(pr, axis=0, keepdims=True)
        acc = a * acc + lax.dot_general(v, pr.astype(BF16), _TN, preferred_element_type=F32)
        return m_new, l, acc

    init = (jnp.full((1, n_heads), NEG, F32), jnp.zeros((1, n_heads), F32),
            jnp.zeros((NSA_KV_COLS, n_heads), F32))
    carry = lax.fori_loop(0, n_pages - 1, lambda p, c: sel_page(p, c, None), init)
    m, l, acc = sel_page(n_pages - 1, carry, kbias_ref[...])
    kvn = kvn_ref[0]
    qmat_f = qmat.astype(F32)

    def new_row(k_col, v_col, extra, m, l, acc):
        s_new = jnp.sum(k_col * qmat_f, axis=0, keepdims=True) + extra
        m_new = jnp.maximum(m, s_new)
        p_new = jnp.where(s_new > HALF_NEG, jnp.exp2(s_new - m_new), 0.0)
        a = jnp.exp2(m - m_new)
        return a * l + p_new, a * acc + v_col * p_new

    new_block = past_len // SEL_BLOCK
    l, acc = new_row(kvn[2 * NSA_KV_COLS:3 * NSA_KV_COLS], kvn[3 * NSA_KV_COLS:4 * NSA_KV_COLS],
                     b0_ref[...] + msk_ref[pl.ds(new_block, 1), :], m, l, acc)
    o_s = acc / jnp.maximum(l, 1e-30)

    win = win_ref[0]
    s_w = jnp.dot(win[:, 0:NSA_KV_COLS].astype(BF16), qmat, preferred_element_type=F32) + wbias_ref[...]
    m_w = jnp.max(s_w, axis=0, keepdims=True)
    p_w = jnp.where(s_w > HALF_NEG, jnp.exp2(s_w - m_w), 0.0)
    l_w = jnp.sum(p_w, axis=0, keepdims=True)
    acc_w = lax.dot_general(win[:, NSA_KV_COLS:half].astype(BF16), p_w.astype(BF16), _TN,
                            preferred_element_type=F32)
    l_w, acc_w = new_row(kvn[4 * NSA_KV_COLS:5 * NSA_KV_COLS], kvn[5 * NSA_KV_COLS:6 * NSA_KV_COLS],
                         b0_ref[...], m_w, l_w, acc_w)
    o_w = acc_w / jnp.maximum(l_w, 1e-30)

    gt = gt_ref[0]
    o_ref[0] = gt[0:1] * o_c + gt[1:2] * o_s + gt[2:3] * o_w


def nsa_decode_attention(page_table, cache, li, qmat, kv_new, gates, win, cmp_w, cover_t, gsum, key_bias,
                         win_bias, bias0):
    bsz, n_pages = page_table.shape
    page = cache.shape[2]
    past_len = n_pages * page
    assert page >= MAX_DISTANCE and page % SEL_BLOCK == 0
    w1p, posb, w1, w2 = cmp_w
    n_selp = cover_t.shape[0]
    full = lambda a: pl.BlockSpec(a.shape, lambda b, pt, _n=a.ndim: (0,) * _n)
    per_b = lambda a: pl.BlockSpec((1,) + a.shape[1:], lambda b, pt, _n=a.ndim: (b,) + (0,) * (_n - 1))
    grid_spec = pltpu.PrefetchScalarGridSpec(
        num_scalar_prefetch=1,
        grid=(bsz,),
        in_specs=[pl.BlockSpec(memory_space=pl.ANY), per_b(qmat), per_b(kv_new), per_b(gates),
                  per_b(win), full(w1p), full(posb), full(w1), full(w2), full(cover_t), full(gsum),
                  full(key_bias), full(win_bias), full(bias0)],
        out_specs=pl.BlockSpec((1, NSA_KV_COLS, NSA_HEADS), lambda b, pt: (b, 0, 0)),
        scratch_shapes=[pltpu.VMEM((NSA_GROUPS, past_len, 2 * NSA_HD), F32),
                        pltpu.VMEM((past_len, 2 * NSA_KV_COLS), F32),
                        pltpu.VMEM((n_selp, NSA_HEADS), F32),
                        pltpu.VMEM((n_selp, NSA_HEADS), F32),
                        pltpu.SemaphoreType.DMA((2,))],
    )
    return pl.pallas_call(
        functools.partial(_nsa_decode_kernel, li=li, n_pages=n_pages, page=page, past_len=past_len),
        out_shape=jax.ShapeDtypeStruct((bsz, NSA_KV_COLS, NSA_HEADS), F32),
        grid_spec=grid_spec,
        compiler_params=_params(("arbitrary",)),
        name="nsa_decode_attention",
    )(page_table, cache, qmat, kv_new, gates, win, w1p, posb, w1, w2, cover_t, gsum, key_bias, win_bias,
      bias0)


def nsa_decode(x, wts, cache, li, win_state, page_table, rel_bias):
    w_q, w_kv, w_g, _, cmp_w = wts
    bsz = x.shape[0]
    n_pool, page = cache.shape[1], cache.shape[2]
    past_len = page_table.shape[1] * page
    wb = win_state.shape[1]
    x2 = x.reshape(bsz, D_MODEL)
    (q,) = linear(x2, w_q, (F32,))
    (kv,) = linear(x2, w_kv, (F32,))
    (gates,) = linear(x2, w_g, (F32,), act="sigmoid", tn=LANES)
    qh = q.reshape(bsz, NSA_GROUPS, NSA_HPG, NSA_HD)
    same_g = jnp.arange(NSA_GROUPS)[:, None] == jnp.arange(NSA_GROUPS)[None, :]
    qmat = jnp.where(same_g[None, :, None, :, None], jnp.transpose(qh, (0, 1, 3, 2))[:, :, :, None, :], 0.0)
    qmat = qmat.reshape(bsz, NSA_KV_COLS, NSA_HEADS).astype(BF16)
    gates = gates[:, :3 * NSA_HEADS].reshape(bsz, 3, NSA_HEADS)

    t = past_len + 1
    n_cmp = (t - CMP_BLOCK) // CMP_STRIDE + 1
    n_chunk = past_len // CMP_STRIDE
    n_sel = -(-t // SEL_BLOCK)
    n_selp = -(-n_sel // LANES) * LANES
    cover_t = _cover_t(n_cmp, n_chunk, n_sel, n_selp)
    head = jnp.arange(NSA_HEADS)
    gsum = (head[:, None] // NSA_HPG == head[None, :] // NSA_HPG).astype(F32)
    key_bias = _key_bias(rel_bias, past_len, past_len - page + jnp.arange(page))
    dist_w = wb - jnp.arange(wb)
    win_bias = _key_bias(rel_bias, past_len, past_len - dist_w)
    win_bias = jnp.where((dist_w <= WINDOW)[:, None], win_bias, NEG)
    bias0 = _key_bias(rel_bias, past_len, jnp.array([past_len]))
    cache2 = cache.reshape(cache.shape[0], n_pool, page, 4 * NSA_KV_COLS)
    win2 = win_state.reshape(bsz, wb, 2 * NSA_KV_COLS)
    o_t = nsa_decode_attention(page_table, cache2, li, qmat, kv[:, :, None], gates, win2, cmp_w, cover_t,
                               gsum, key_bias, win_bias, bias0)
    o_t = o_t.reshape(bsz, NSA_GROUPS, NSA_HD, NSA_GROUPS, NSA_HPG)
    o = jnp.stack([o_t[:, g, :, g, :] for g in range(NSA_GROUPS)], axis=1)
    o = jnp.transpose(o, (0, 1, 3, 2)).reshape(bsz, NSA_Q_COLS)
    new_kv = kv[:, :4 * NSA_KV_COLS].reshape(bsz, 1, 4, NSA_GROUPS, NSA_HD)
    new_win = kv[:, 4 * NSA_KV_COLS:].reshape(bsz, 1, 2, NSA_GROUPS, NSA_HD)
    win_all = jnp.concatenate([win_state, new_win], axis=1)
    keep = min(WINDOW, wb + 1)
    return o, new_kv, win_all[:, wb + 1 - keep:]


def kernel(x_prompt, x_sample, cache_nsa_kv, state_nsa_win, cache_diff_kv, page_table, rel_bias,
           nsa_w_in, nsa_w_out, nsa_cmp_pos, nsa_cmp_w1, nsa_cmp_w2,
           diff_w_in, diff_w_out, diff_lambda, diff_subln_g,
           mlp_w_up, mlp_w_down, ln_g, ln_b):
    bp, sp, d = x_prompt.shape
    bs = x_sample.shape[0]
    xp = x_prompt.reshape(bp * sp, d)
    xs = x_sample.reshape(bs, d)
    nsa_bias = _tiles_t(_bias_tiles(rel_bias, NSA_TILE), NSA_HPG)
    diff_bias = _tiles_t(_bias_tiles(rel_bias, DIFF_TILE)[:2], 2)
    nsa_kv_p, nsa_kv_s, nsa_win_p, nsa_win_s, diff_kv_p, diff_kv_s = [], [], [], [], [], []
    for i in range(DEPTH):
        li = i // N_MIXERS
        if i % N_MIXERS == 0:
            wts = _nsa_weights(nsa_w_in[li], nsa_w_out[li], nsa_cmp_pos[li], nsa_cmp_w1[li], nsa_cmp_w2[li])
            op, kvp, wp = nsa_prompt(xp.reshape(bp, sp, d), wts, nsa_bias)
            os_, kvs, wsn = nsa_decode(xs.reshape(bs, 1, d), wts, cache_nsa_kv, li, state_nsa_win[li],
                                       page_table, rel_bias)
            nsa_kv_p.append(kvp)
            nsa_kv_s.append(kvs)
            nsa_win_p.append(wp)
            nsa_win_s.append(wsn)
            w_out = wts[3]
        else:
            lambda_init = 0.8 - 0.6 * math.exp(-0.3 * i)
            wts = _diff_weights(diff_w_in[li], diff_w_out[li])
            op, kvp = diff_prompt(xp.reshape(bp, sp, d), wts, diff_lambda[li], diff_subln_g[li], diff_bias,
                                  lambda_init)
            os_, kvs = diff_decode(xs.reshape(bs, 1, d), wts, cache_diff_kv, li, page_table, rel_bias,
                                   diff_lambda[li], diff_subln_g[li], lambda_init)
            diff_kv_p.append(kvp)
            diff_kv_s.append(kvs)
            w_out = wts[2]
        w_up, w_down = mlp_w_up[i].astype(BF16), mlp_w_down[i].astype(BF16)
        xp = linear_post_norm(op, w_out, xp, ln_g[i, 0], ln_b[i, 0])
        xs = linear_post_norm(os_, w_out, xs, ln_g[i, 0], ln_b[i, 0])
        xp = mlp_post_norm(xp, w_up, w_down, ln_g[i, 1], ln_b[i, 1])
        xs = mlp_post_norm(xs, w_up, w_down, ln_g[i, 1], ln_b[i, 1])
    return (xp.reshape(bp, sp, d), xs.reshape(bs, 1, d), jnp.stack(nsa_kv_p), jnp.stack(nsa_kv_s),
            jnp.stack(nsa_win_p), jnp.stack(nsa_win_s), jnp.stack(diff_kv_p), jnp.stack(diff_kv_s))
```

```python
import functools
import math

import jax
import jax.numpy as jnp
from jax import lax
from jax.experimental import pallas as pl
from jax.experimental.pallas import tpu as pltpu

F32 = jnp.float32
BF16 = jnp.bfloat16

D_MODEL = 1024
DEPTH = 4
N_MIXERS = 2
N_BUCKETS = 32
MAX_DISTANCE = 128
NSA_HEADS = 16
NSA_GROUPS = 4
NSA_HPG = NSA_HEADS // NSA_GROUPS
NSA_HD = D_MODEL // NSA_HEADS
CMP_BLOCK = 32
CMP_STRIDE = 16
CMP_HIDDEN = 2 * NSA_HD
SEL_BLOCK = 64
SEL_TOPK = 16
WINDOW = 512
FORCE_BONUS = 1e4
NSA_KV_COLS = NSA_GROUPS * NSA_HD
NSA_Q_COLS = NSA_HEADS * NSA_HD
DIFF_HEADS = 8
DIFF_D = D_MODEL // (2 * DIFF_HEADS)
DIFF_VD = 2 * DIFF_D
DIFF_Q_COLS = DIFF_HEADS * DIFF_VD
D_FF = 4 * D_MODEL
ALPHA = (2 * DEPTH) ** 0.25
LN_EPS = 1e-5
NEG = -1e30
HALF_NEG = -5e29

LOG2E = math.log2(math.e)
LANES = 128
NSA_TILE = 256
DIFF_TILE = 512
PAGES_PER_STEP = 8
NSA_PAGES_PER_CHUNK = 8
VMEM_LIMIT = 56 * 1024 * 1024

_NT = (((1,), (1,)), ((), ()))
_TN = (((0,), (0,)), ((), ()))


def _params(sem, vmem=VMEM_LIMIT):
    return pltpu.CompilerParams(dimension_semantics=sem, vmem_limit_bytes=vmem)


def _linear_kernel(a_ref, w_ref, *o_refs, act):
    y = jnp.dot(a_ref[...].astype(BF16), w_ref[...], preferred_element_type=F32)
    if act == "sigmoid":
        y = jax.nn.sigmoid(y)
    for o_ref in o_refs:
        o_ref[...] = y.astype(o_ref.dtype)


def linear(a, w, out_dtypes, act=None, tm=512, tn=512):
    m, k = a.shape
    n = w.shape[1]
    tm, tn = min(tm, m), min(tn, n)
    assert m % tm == 0 and n % tn == 0
    outs = pl.pallas_call(
        functools.partial(_linear_kernel, act=act),
        out_shape=[jax.ShapeDtypeStruct((m, n), dt) for dt in out_dtypes],
        grid=(m // tm, n // tn),
        in_specs=[pl.BlockSpec((tm, k), lambda i, j: (i, 0)),
                  pl.BlockSpec((k, tn), lambda i, j: (0, j))],
        out_specs=[pl.BlockSpec((tm, tn), lambda i, j: (i, j)) for _ in out_dtypes],
        compiler_params=_params(("parallel", "parallel")),
        name="linear",
    )(a, w)
    return outs


def _post_norm(resid, sub, g, b):
    z = ALPHA * resid + sub
    mu = jnp.mean(z, axis=-1, keepdims=True)
    zc = z - mu
    var = jnp.mean(zc * zc, axis=-1, keepdims=True)
    return zc * lax.rsqrt(var + LN_EPS) * g + b


def _linear_ln_kernel(a_ref, w_ref, r_ref, g_ref, b_ref, o_ref):
    y = jnp.dot(a_ref[...].astype(BF16), w_ref[...], preferred_element_type=F32)
    o_ref[...] = _post_norm(r_ref[...], y, g_ref[...], b_ref[...])


def linear_post_norm(a, w, resid, g, b, tm=512):
    m, k = a.shape
    n = w.shape[1]
    tm = min(tm, m)
    assert m % tm == 0
    return pl.pallas_call(
        _linear_ln_kernel,
        out_shape=jax.ShapeDtypeStruct((m, n), F32),
        grid=(m // tm,),
        in_specs=[pl.BlockSpec((tm, k), lambda i: (i, 0)),
                  pl.BlockSpec((k, n), lambda i: (0, 0)),
                  pl.BlockSpec((tm, n), lambda i: (i, 0)),
                  pl.BlockSpec((1, n), lambda i: (0, 0)),
                  pl.BlockSpec((1, n), lambda i: (0, 0))],
        out_specs=pl.BlockSpec((tm, n), lambda i: (i, 0)),
        compiler_params=_params(("parallel",)),
        name="linear_post_norm",
    )(a, w, resid, g.reshape(1, n), b.reshape(1, n))


def _mlp_kernel(x_ref, wu_ref, wd_ref, g_ref, b_ref, o_ref, acc_ref):
    f = pl.program_id(1)

    @pl.when(f == 0)
    def _():
        acc_ref[...] = jnp.zeros_like(acc_ref)

    h = jnp.dot(x_ref[...].astype(BF16), wu_ref[...], preferred_element_type=F32)
    h = jnp.square(jnp.maximum(h, 0.0))
    acc_ref[...] += jnp.dot(h.astype(BF16), wd_ref[...], preferred_element_type=F32)

    @pl.when(f == pl.num_programs(1) - 1)
    def _():
        o_ref[...] = _post_norm(x_ref[...], acc_ref[...], g_ref[...], b_ref[...])


def mlp_post_norm(x, w_up, w_down, g, b, tm=512, tf=512):
    m, d = x.shape
    ff = w_up.shape[1]
    tm = min(tm, m)
    assert m % tm == 0 and ff % tf == 0
    return pl.pallas_call(
        _mlp_kernel,
        out_shape=jax.ShapeDtypeStruct((m, d), F32),
        grid=(m // tm, ff // tf),
        in_specs=[pl.BlockSpec((tm, d), lambda i, f: (i, 0)),
                  pl.BlockSpec((d, tf), lambda i, f: (0, f)),
                  pl.BlockSpec((tf, d), lambda i, f: (f, 0)),
                  pl.BlockSpec((1, d), lambda i, f: (0, 0)),
                  pl.BlockSpec((1, d), lambda i, f: (0, 0))],
        out_specs=pl.BlockSpec((tm, d), lambda i, f: (i, 0)),
        scratch_shapes=[pltpu.VMEM((tm, d), F32)],
        compiler_params=_params(("parallel", "arbitrary")),
        name="mlp_post_norm",
    )(x, w_up, w_down, g.reshape(1, d), b.reshape(1, d))


def _t5_bucket(dist):
    n = jnp.maximum(dist, 0)
    max_exact = N_BUCKETS // 2
    large = max_exact + (jnp.log(jnp.maximum(n, 1).astype(F32) / max_exact)
                         / math.log(MAX_DISTANCE / max_exact) * (N_BUCKETS - max_exact)).astype(jnp.int32)
    return jnp.where(n < max_exact, n, jnp.minimum(large, N_BUCKETS - 1))


def _bias_tiles(rel_bias, t):
    assert t >= MAX_DISTANCE
    r = jnp.arange(t)[:, None]
    c = jnp.arange(t)[None, :]

    def rel(off):
        k = jnp.arange(2 * t)
        dist = jnp.where(k < t, off - k, off + 2 * t - k)
        u = _key_bias(rel_bias, 0, -dist).T
        flat = jnp.tile(u, (1, t))[:, :t * (2 * t - 1)]
        return flat.reshape(-1, t, 2 * t - 1)[:, :, :t]

    prev = rel(t)
    diag = jnp.where((c <= r)[None], rel(0), NEG)
    edge = jnp.where((c >= r)[None], jnp.zeros_like(prev), NEG)
    return prev, diag, edge


def _tiles_t(tiles, maps_per_step):
    x = jnp.stack(tiles)
    k, maps, t, _ = x.shape
    x = x.reshape(k, maps // maps_per_step, maps_per_step, t, t)
    return jnp.transpose(x, (0, 1, 4, 2, 3)).reshape(k, maps // maps_per_step, t, maps_per_step * t)


def _key_bias(rel_bias, q_pos, k_pos):
    return (rel_bias[_t5_bucket(q_pos - k_pos)] - rel_bias[N_BUCKETS - 1][None, :]) * LOG2E


def _softmax_step(s, v, m, l, acc):
    m_new = jnp.maximum(m, jnp.max(s, axis=0, keepdims=True))
    p = jnp.exp2(s - m_new)
    a = jnp.exp2(m - m_new)
    l = a * l + jnp.sum(p, axis=0, keepdims=True)
    acc = a * acc + lax.dot_general(v, p.astype(BF16), _TN, preferred_element_type=F32)
    return m_new, l, acc


def _compress_pair(x_ref, n_chunk, w1_ref, pos_term, w2):
    acc = jnp.zeros((n_chunk, 4 * CMP_HIDDEN), F32)
    for s in range(0, CMP_STRIDE, 2):
        xs = jnp.concatenate([x_ref[pl.ds(s, n_chunk, stride=CMP_STRIDE), :],
                              x_ref[pl.ds(s + 1, n_chunk, stride=CMP_STRIDE), :]], axis=1)
        acc = acc + jnp.dot(xs.astype(BF16), w1_ref[s // 2], preferred_element_type=F32)
    outs = []
    for g2 in range(2):
        first = acc[:, g2 * 2 * CMP_HIDDEN: g2 * 2 * CMP_HIDDEN + CMP_HIDDEN]
        second = acc[:, g2 * 2 * CMP_HIDDEN + CMP_HIDDEN: (g2 + 1) * 2 * CMP_HIDDEN]
        second_next = jnp.concatenate([second[1:], second[:1]], axis=0)
        hid = first + second_next + pos_term
        outs.append(jnp.dot(jax.nn.gelu(hid).astype(BF16), w2, preferred_element_type=F32))
    return outs


def _compress_weights(pos, w1, w2):
    n_sub = CMP_BLOCK // CMP_STRIDE
    w1r = w1.reshape(2, n_sub, CMP_STRIDE, NSA_HD, CMP_HIDDEN)
    w1t = jnp.transpose(w1r, (0, 2, 3, 1, 4)).reshape(2, CMP_STRIDE, NSA_HD, n_sub * CMP_HIDDEN)
    zeros = jnp.zeros_like(w1t)
    w1p = jnp.concatenate([jnp.concatenate([w1t, zeros], -1), jnp.concatenate([zeros, w1t], -1)], axis=2)
    w1p = w1p.reshape(2, CMP_STRIDE // 2, 4 * NSA_HD, 2 * n_sub * CMP_HIDDEN)
    posb = jnp.broadcast_to(pos.reshape(2, 1, CMP_BLOCK * NSA_HD), (2, 8, CMP_BLOCK * NSA_HD))
    return w1p.astype(BF16), posb.astype(BF16), w1.astype(BF16), w2.astype(BF16)


def _pos_term(posb_ref, w1_ref):
    return jnp.dot(posb_ref[...], w1_ref[...], preferred_element_type=F32)[0:1]


def _compress_prompt_kernel(x_ref, w1p_ref, posb_ref, w1_ref, w2_ref, o_ref, *, n_chunk):
    pos_term = _pos_term(posb_ref.at[0], w1_ref.at[0])
    outs = _compress_pair(x_ref.at[0], n_chunk, w1p_ref.at[0], pos_term, w2_ref[0])
    for g2 in range(2):
        o_ref[0, 0, g2] = outs[g2].astype(o_ref.dtype)


def compress_prompt(kv, cmp_w):
    w1p, posb, w1, w2 = cmp_w
    bsz, s, _ = kv.shape
    n_chunk = s // CMP_STRIDE
    n_pair = NSA_GROUPS // 2
    return pl.pallas_call(
        functools.partial(_compress_prompt_kernel, n_chunk=n_chunk),
        out_shape=jax.ShapeDtypeStruct((2, bsz, NSA_GROUPS, n_chunk, NSA_HD), BF16),
        grid=(2, bsz, n_pair),
        in_specs=[pl.BlockSpec((1, s, 2 * NSA_HD), lambda t, b, pr: (b, 0, t * n_pair + pr)),
                  pl.BlockSpec((1,) + w1p.shape[1:], lambda t, b, pr: (t, 0, 0, 0)),
                  pl.BlockSpec((1,) + posb.shape[1:], lambda t, b, pr: (t, 0, 0)),
                  pl.BlockSpec((1,) + w1.shape[1:], lambda t, b, pr: (t, 0, 0)),
                  pl.BlockSpec((1,) + w2.shape[1:], lambda t, b, pr: (t, 0, 0))],
        out_specs=pl.BlockSpec((1, 1, 2, n_chunk, NSA_HD), lambda t, b, pr: (t, b, pr, 0, 0)),
        compiler_params=_params(("parallel", "parallel", "parallel")),
        name="nsa_compress_prompt",
    )(kv, w1p, posb, w1, w2)


def _selection_mask_t(imp_t, q_pos, n_valid_rows, sc_ref):
    j = lax.broadcasted_iota(jnp.int32, imp_t.shape, 0)
    cur = lax.shift_right_arithmetic(q_pos, SEL_BLOCK.bit_length() - 1)
    valid = j <= cur
    forced = (j == 0) | (j == cur) | (j == cur - 1)
    score = jnp.where(valid, imp_t + jnp.where(forced, FORCE_BONUS, 0.0), NEG)
    sc_ref[...] = score

    def body(jp, cnt):
        row = sc_ref[pl.ds(jp, 1), :]
        beats = (row > score) | ((row == score) & (jp < j))
        return cnt + jnp.where(beats, 1.0, 0.0)

    cnt = lax.fori_loop(0, n_valid_rows, body, jnp.zeros(imp_t.shape, F32))
    return jnp.where((cnt < SEL_TOPK) & valid, 0.0, NEG)


def _nsa_prompt_kernel(q_ref, kc_ref, vc_ref, ks_ref, vs_ref, kw_ref, vw_ref, gt_ref, cov_ref, bias_ref,
                       o_ref, sc_ref, *, n_cmp):
    t = NSA_TILE
    qi = pl.program_id(2)
    q0 = qi * t
    rows = NSA_HPG * t
    q4 = q_ref[0].reshape(rows, LANES)

    n_pad = kc_ref.shape[2]
    s_c = lax.dot_general(kc_ref[0, 0], q4, _NT, preferred_element_type=F32)
    q_pos = q0 + (lax.broadcasted_iota(jnp.int32, (n_pad, rows), 1) & (t - 1))
    n = lax.broadcasted_iota(jnp.int32, (n_pad, rows), 0)
    mask_c = (n * CMP_STRIDE + CMP_BLOCK - 1 <= q_pos) & (n < n_cmp)
    s_c = jnp.where(mask_c, s_c, NEG)
    e = jnp.where(mask_c, jnp.exp2(s_c - jnp.max(s_c, axis=0, keepdims=True)), 0.0)
    p_c = e / jnp.maximum(jnp.sum(e, axis=0, keepdims=True), 1e-30)
    o_c = lax.dot_general(vc_ref[0, 0], p_c.astype(BF16), _TN, preferred_element_type=F32)
    p_sum = p_c[:, 0:t]
    for i in range(1, NSA_HPG):
        p_sum = p_sum + p_c[:, i * t:(i + 1) * t]
    imp_t = jnp.dot(cov_ref[...], p_sum, precision=lax.Precision.HIGHEST,
                    preferred_element_type=F32)

    n_sel = imp_t.shape[0]
    qp_t = q0 + lax.broadcasted_iota(jnp.int32, (n_sel, t), 1)
    msk_t = _selection_mask_t(imp_t, qp_t, (q0 + t - 1) // SEL_BLOCK + 1, sc_ref)
    pad = [jnp.zeros((LANES - NSA_HD - n_sel, t), F32)] if n_sel < LANES - NSA_HD else []
    msk = jnp.concatenate([jnp.zeros((NSA_HD, t), F32), msk_t] + pad, axis=0).T
    q_aug = q4 + jnp.concatenate([msk.astype(BF16)] * NSA_HPG, axis=0)

    init = (jnp.full((1, rows), NEG, F32), jnp.zeros((1, rows), F32), jnp.zeros((NSA_HD, rows), F32))
    prev_tile, diag_tile, edge_tile = 0, 1, 2

    def step(q, k_ref, v_ref, chunk, width, tile, carry):
        k0 = pl.multiple_of(chunk * t, t)
        s = lax.dot_general(k_ref[0, 0, pl.ds(k0, width), :], q, _NT, preferred_element_type=F32)
        if tile is not None:
            s = s + bias_ref[tile, 0]
        return _softmax_step(s, v_ref[0, 0, pl.ds(k0, width), :], *carry)

    def maybe(pred, fn, carry):
        return lax.cond(pred, fn, lambda cr: cr, carry)

    n_far = jnp.maximum(qi - 1, 0)
    carry = lax.fori_loop(0, n_far // 2, lambda c, cr: step(q_aug, ks_ref, vs_ref, 2 * c, 2 * t, None, cr), init)
    carry = maybe(n_far % 2 == 1, lambda cr: step(q_aug, ks_ref, vs_ref, n_far - 1, t, None, cr), carry)
    carry = maybe(qi >= 1, lambda cr: step(q_aug, ks_ref, vs_ref, qi - 1, t, prev_tile, cr), carry)
    _, l_s, acc_s = step(q_aug, ks_ref, vs_ref, qi, t, diag_tile, carry)

    carry = maybe(qi >= 2, lambda cr: step(q4, kw_ref, vw_ref, qi - 2, t, edge_tile, cr), init)
    carry = maybe(qi >= 1, lambda cr: step(q4, kw_ref, vw_ref, qi - 1, t, prev_tile, cr), carry)
    _, l_w, acc_w = step(q4, kw_ref, vw_ref, qi, t, diag_tile, carry)

    gt = gt_ref[0, 0, 0]
    o = (gt[0:1] * o_c + (gt[1:2] / jnp.maximum(l_s, 1e-30)) * acc_s
         + (gt[2:3] / jnp.maximum(l_w, 1e-30)) * acc_w)
    halves = [jnp.concatenate([o[:, (2 * k) * t:(2 * k + 1) * t], o[:, (2 * k + 1) * t:(2 * k + 2) * t]],
                              axis=0).T for k in range(NSA_HPG // 2)]
    o_ref[0] = jnp.concatenate(halves, axis=1).astype(o_ref.dtype)


def nsa_prompt_attention(qh, kc, vc, ks, vs, kw, vw, gates, cover_t, bias, n_cmp):
    bsz, _, s, _ = qh.shape
    t = NSA_TILE
    assert s % (2 * t) == 0 and WINDOW == 2 * t and NSA_HPG % 2 == 0
    n_sel = cover_t.shape[0]
    rows = NSA_HPG * t
    per_bg = lambda b, g, i: (b, g, 0, 0)
    return pl.pallas_call(
        functools.partial(_nsa_prompt_kernel, n_cmp=n_cmp),
        out_shape=jax.ShapeDtypeStruct((bsz, s, NSA_Q_COLS), BF16),
        grid=(bsz, NSA_GROUPS, s // t),
        in_specs=[pl.BlockSpec((1, NSA_HPG, t, LANES), lambda b, g, i: (b, g, i, 0)),
                  pl.BlockSpec((1, 1) + kc.shape[2:], per_bg),
                  pl.BlockSpec((1, 1) + vc.shape[2:], per_bg),
                  pl.BlockSpec((1, 1) + ks.shape[2:], per_bg),
                  pl.BlockSpec((1, 1) + vs.shape[2:], per_bg),
                  pl.BlockSpec((1, 1) + kw.shape[2:], per_bg),
                  pl.BlockSpec((1, 1) + vw.shape[2:], per_bg),
                  pl.BlockSpec((1, 1, 1, 3, rows), lambda b, g, i: (b, g, i, 0, 0)),
                  pl.BlockSpec(cover_t.shape, lambda b, g, i: (0, 0)),
                  pl.BlockSpec((3, 1, t, rows), lambda b, g, i: (0, g, 0, 0))],
        out_specs=pl.BlockSpec((1, t, NSA_HPG * NSA_HD), lambda b, g, i: (b, i, g)),
        scratch_shapes=[pltpu.VMEM((n_sel, t), F32)],
        compiler_params=_params(("parallel", "parallel", "arbitrary")),
        name="nsa_prompt_attention",
    )(qh, kc, vc, ks, vs, kw, vw, gates, cover_t, bias)


def _cover_t(n_cmp, n_cmp_pad, n_sel, n_sel_pad):
    c_start = jnp.arange(n_cmp_pad)[None, :] * CMP_STRIDE
    s_start = jnp.arange(n_sel_pad)[:, None] * SEL_BLOCK
    cover = (c_start < s_start + SEL_BLOCK) & (c_start + CMP_BLOCK > s_start)
    cover = cover & (jnp.arange(n_cmp_pad)[None, :] < n_cmp) & (jnp.arange(n_sel_pad)[:, None] < n_sel)
    return cover.astype(F32)


def _nsa_weights(w_in, w_out, pos, w1, w2):
    scale = NSA_HD ** -0.5 * LOG2E
    kv0 = NSA_Q_COLS
    g0 = NSA_Q_COLS + 6 * NSA_KV_COLS
    w_q = (w_in[:, :kv0] * scale).astype(BF16)
    w_kv = w_in[:, kv0:g0].astype(BF16)
    w_g = jnp.pad(w_in[:, g0:], ((0, 0), (0, LANES - 3 * NSA_HEADS))).astype(BF16)
    return w_q, w_kv, w_g, w_out.astype(BF16), _compress_weights(pos, w1, w2)


def nsa_prompt(x, wts, bias):
    w_q, w_kv, w_g, _, cmp_w = wts
    bsz, s, d = x.shape
    x2 = x.reshape(bsz * s, d)
    (q,) = linear(x2, w_q, (BF16,))
    kv32, kv16 = linear(x2, w_kv, (F32, BF16))
    (gates,) = linear(x2, w_g, (F32,), act="sigmoid", tn=LANES)
    kv32 = kv32.reshape(bsz, s, 6 * NSA_KV_COLS)
    new_kv = kv32[..., :4 * NSA_KV_COLS].reshape(bsz, s, 4, NSA_GROUPS, NSA_HD)
    new_win = kv32[..., 4 * NSA_KV_COLS:].reshape(bsz, s, 2, NSA_GROUPS, NSA_HD)

    n_cmp = (s - CMP_BLOCK) // CMP_STRIDE + 1
    n_chunk = s // CMP_STRIDE
    n_sel = s // SEL_BLOCK
    cmp = compress_prompt(kv32, cmp_w)
    kc = jnp.pad(cmp[0], ((0, 0), (0, 0), (0, 0), (0, LANES - NSA_HD)))
    vc = cmp[1]

    qh = q.reshape(bsz, s, NSA_HEADS, NSA_HD).transpose(0, 2, 1, 3)
    qh = jnp.pad(qh, ((0, 0), (0, 0), (0, 0), (0, LANES - NSA_HD)))
    kvh = kv16.reshape(bsz, s, 6, NSA_GROUPS, NSA_HD).transpose(2, 0, 3, 1, 4)
    onehot = (jnp.arange(s)[:, None] // SEL_BLOCK == jnp.arange(LANES - NSA_HD)[None, :]).astype(BF16)
    ks = jnp.concatenate([kvh[2], jnp.broadcast_to(onehot, (bsz, NSA_GROUPS) + onehot.shape)], axis=-1)
    kw = jnp.pad(kvh[4], ((0, 0), (0, 0), (0, 0), (0, LANES - NSA_HD)))
    nq = s // NSA_TILE
    gates = gates[:, :3 * NSA_HEADS].reshape(bsz, nq, NSA_TILE, 3, NSA_GROUPS, NSA_HPG)
    gates = gates.transpose(0, 4, 1, 3, 5, 2).reshape(bsz, NSA_GROUPS, nq, 3, NSA_HPG * NSA_TILE)
    cover_t = _cover_t(n_cmp, n_chunk, n_sel, n_sel)
    o = nsa_prompt_attention(qh, kc, vc, ks, kvh[3], kw, kvh[5], gates, cover_t, bias, n_cmp)
    o = o.reshape(bsz * s, NSA_Q_COLS)
    keep = min(WINDOW, s)
    return o, new_kv, new_win[:, s - keep:]


def _lambda_full(lam_ref, lambda_init):
    lf = lam_ref[...]
    a = jnp.sum(lf[0:1] * lf[1:2], axis=-1, keepdims=True)
    b = jnp.sum(lf[2:3] * lf[3:4], axis=-1, keepdims=True)
    return jnp.exp(a) - jnp.exp(b) + lambda_init


def _sub_norm(o, g, lambda_init):
    return o * lax.rsqrt(jnp.mean(o * o, axis=-1, keepdims=True) + LN_EPS) * g * (1.0 - lambda_init)


def _diff_prompt_kernel(q_ref, k_ref, v_ref, bias_ref, lam_ref, g_ref, o_ref, *, lambda_init):
    t = DIFF_TILE
    qi = pl.program_id(2)
    q = q_ref[0]
    lane = lax.broadcasted_iota(jnp.int32, q.shape, 1)
    zero = jnp.zeros_like(q)
    q2 = jnp.concatenate([jnp.where(lane < DIFF_D, q, zero), jnp.where(lane >= DIFF_D, q, zero)], axis=0)
    rows = 2 * t
    init = (jnp.full((1, rows), NEG, F32), jnp.zeros((1, rows), F32), jnp.zeros((DIFF_VD, rows), F32))

    def step(chunk, tile, carry):
        k0 = pl.multiple_of(chunk * t, t)
        s = lax.dot_general(k_ref[0, pl.ds(k0, t), :], q2, _NT, preferred_element_type=F32)
        if tile is not None:
            s = s + bias_ref[tile, 0]
        return _softmax_step(s, v_ref[0, pl.ds(k0, t), :], *carry)

    carry = lax.fori_loop(0, jnp.maximum(qi - 1, 0), lambda c, cr: step(c, None, cr), init)
    carry = lax.cond(qi >= 1, lambda cr: step(qi - 1, 0, cr), lambda cr: cr, carry)
    _, l, acc = step(qi, 1, carry)
    o = acc / jnp.maximum(l, 1e-30)
    o = o[:, :t] - _lambda_full(lam_ref, lambda_init) * o[:, t:]
    o = o * lax.rsqrt(jnp.mean(o * o, axis=0, keepdims=True) + LN_EPS) * g_ref[...] * (1.0 - lambda_init)
    o_ref[0] = o.T.astype(o_ref.dtype)


def diff_prompt_attention(q, kv, bias, lam, g, lambda_init):
    bsz, s, _ = q.shape
    t = DIFF_TILE
    assert s % t == 0
    return pl.pallas_call(
        functools.partial(_diff_prompt_kernel, lambda_init=lambda_init),
        out_shape=jax.ShapeDtypeStruct((bsz, s, DIFF_Q_COLS), BF16),
        grid=(bsz, DIFF_HEADS, s // t),
        in_specs=[pl.BlockSpec((1, t, DIFF_VD), lambda b, h, i: (b, i, h)),
                  pl.BlockSpec((1, s, DIFF_VD), lambda b, h, i: (b, 0, h)),
                  pl.BlockSpec((1, s, DIFF_VD), lambda b, h, i: (b, 0, DIFF_HEADS + h)),
                  pl.BlockSpec((2, 1, t, 2 * t), lambda b, h, i: (0, h, 0, 0)),
                  pl.BlockSpec(lam.shape, lambda b, h, i: (0, 0)),
                  pl.BlockSpec((DIFF_VD, 1), lambda b, h, i: (0, 0))],
        out_specs=pl.BlockSpec((1, t, DIFF_VD), lambda b, h, i: (b, i, h)),
        compiler_params=_params(("parallel", "parallel", "arbitrary")),
        name="diff_prompt_attention",
    )(q, kv, kv, bias, lam, g.reshape(DIFF_VD, 1))


def _diff_weights(w_in, w_out):
    scale = DIFF_D ** -0.5 * LOG2E
    return (w_in[:, :DIFF_Q_COLS] * scale).astype(BF16), w_in[:, DIFF_Q_COLS:].astype(BF16), w_out.astype(BF16)


def diff_prompt(x, wts, lam, g, bias, lambda_init):
    w_q, w_kv, _ = wts
    bsz, s, d = x.shape
    x2 = x.reshape(bsz * s, d)
    (q,) = linear(x2, w_q, (BF16,))
    kv32, kv16 = linear(x2, w_kv, (F32, BF16))
    o = diff_prompt_attention(q.reshape(bsz, s, -1), kv16.reshape(bsz, s, -1), bias, lam, g, lambda_init)
    return o.reshape(bsz * s, DIFF_Q_COLS), kv32.reshape(bsz, s, 2, DIFF_HEADS, DIFF_VD)


def _diff_decode_kernel(pt_ref, q_ref, *refs, lambda_init, n_pg):
    c_refs = refs[:n_pg]
    bias_ref, kn_ref, vn_ref, b0_ref, lam_ref, g_ref, o_ref, m_ref, l_ref, acc_ref = refs[n_pg:]
    p = pl.program_id(1)
    last = pl.num_programs(1) - 1

    @pl.when(p == 0)
    def _():
        m_ref[...] = jnp.full_like(m_ref, NEG)
        l_ref[...] = jnp.zeros_like(l_ref)
        acc_ref[...] = jnp.zeros_like(acc_ref)

    q2 = q_ref[0]
    ks = [c_ref[:, 0].reshape(-1, DIFF_VD).astype(BF16) for c_ref in c_refs]
    vs = [c_ref[:, 1].reshape(-1, DIFF_VD).astype(BF16) for c_ref in c_refs]
    s = jnp.concatenate([lax.dot_general(q2, k, _NT, preferred_element_type=F32) for k in ks], axis=1)
    s = s + bias_ref[jnp.where(p == last, 1, 0)]
    m, l, acc = m_ref[...], l_ref[...], acc_ref[...]
    m_new = jnp.maximum(m, jnp.max(s, axis=-1, keepdims=True))
    pr = jnp.exp2(s - m_new)
    a = jnp.exp2(m - m_new)
    m, l = m_new, a * l + jnp.sum(pr, axis=-1, keepdims=True)
    acc = a * acc + jnp.dot(pr.astype(BF16), jnp.concatenate(vs, axis=0), preferred_element_type=F32)
    m_ref[...] = m
    l_ref[...] = l
    acc_ref[...] = acc

    @pl.when(p == last)
    def _():
        s_new = jnp.sum(q2.astype(F32) * kn_ref[0], axis=-1, keepdims=True) + b0_ref[...]
        m_new = jnp.maximum(m, s_new)
        a = jnp.exp2(m - m_new)
        p_new = jnp.exp2(s_new - m_new)
        o_n = (a * acc + p_new * vn_ref[0]) / jnp.maximum(a * l + p_new, 1e-30)
        lam_full = _lambda_full(lam_ref, lambda_init)
        for h in range(DIFF_HEADS):
            o_h = o_n[2 * h:2 * h + 1] - lam_full * o_n[2 * h + 1:2 * h + 2]
            o_ref[0, h:h + 1, :] = _sub_norm(o_h, g_ref[...], lambda_init)


def diff_decode_attention(page_table, q2, cache, li, bias, k_new, v_new, bias0, lam, g, lambda_init):
    bsz, n_pages = page_table.shape
    page = cache.shape[2]
    n_pg = PAGES_PER_STEP
    assert n_pages % n_pg == 0
    n_maps = 2 * DIFF_HEADS
    per_b = lambda b, p, pt: (b, 0, 0)

    def page_spec(k):
        return pl.BlockSpec((None, None, page, 2, DIFF_HEADS, DIFF_VD),
                            lambda b, p, pt: (li, pt[b, p * n_pg + k], 0, 0, 0, 0))

    grid_spec = pltpu.PrefetchScalarGridSpec(
        num_scalar_prefetch=1,
        grid=(bsz, n_pages // n_pg),
        in_specs=[pl.BlockSpec((1, n_maps, DIFF_VD), per_b)] + [page_spec(k) for k in range(n_pg)]
        + [pl.BlockSpec(bias.shape, lambda b, p, pt: (0, 0, 0)),
           pl.BlockSpec((1, n_maps, DIFF_VD), per_b),
           pl.BlockSpec((1, n_maps, DIFF_VD), per_b),
           pl.BlockSpec((n_maps, 1), lambda b, p, pt: (0, 0)),
           pl.BlockSpec(lam.shape, lambda b, p, pt: (0, 0)),
           pl.BlockSpec((1, DIFF_VD), lambda b, p, pt: (0, 0))],
        out_specs=pl.BlockSpec((1, DIFF_HEADS, DIFF_VD), per_b),
        scratch_shapes=[pltpu.VMEM((n_maps, 1), F32), pltpu.VMEM((n_maps, 1), F32),
                        pltpu.VMEM((n_maps, DIFF_VD), F32)],
    )
    return pl.pallas_call(
        functools.partial(_diff_decode_kernel, lambda_init=lambda_init, n_pg=n_pg),
        out_shape=jax.ShapeDtypeStruct((bsz, DIFF_HEADS, DIFF_VD), F32),
        grid_spec=grid_spec,
        compiler_params=_params(("parallel", "arbitrary")),
        name="diff_decode_attention",
    )(page_table, q2, *([cache] * n_pg), bias, k_new, v_new, bias0, lam, g.reshape(1, DIFF_VD))


def diff_decode(x, wts, cache, li, page_table, rel_bias, lam, g, lambda_init):
    w_q, w_kv, _ = wts
    bsz = x.shape[0]
    page = cache.shape[2]
    past_len = page_table.shape[1] * page
    x2 = x.reshape(bsz, D_MODEL)
    (q,) = linear(x2, w_q, (F32,))
    (kv,) = linear(x2, w_kv, (F32,))
    n_maps = 2 * DIFF_HEADS
    per_map = lambda a: jnp.repeat(a.reshape(bsz, DIFF_HEADS, DIFF_VD), 2, axis=1)
    own_lanes = jnp.arange(DIFF_VD)[None, :] // DIFF_D == jnp.arange(n_maps)[:, None] % 2
    q2 = jnp.where(own_lanes[None], per_map(q), 0.0).astype(BF16)
    n_keys = PAGES_PER_STEP * page
    assert n_keys >= MAX_DISTANCE
    col_head = jnp.arange(n_keys * DIFF_HEADS)[None, :] % DIFF_HEADS
    head_mask = jnp.where(col_head == jnp.arange(n_maps)[:, None] // 2, 0.0, NEG)
    near = _key_bias(rel_bias, past_len, past_len - n_keys + jnp.arange(n_keys))
    bias = jnp.stack([head_mask, head_mask + jnp.repeat(near.T, DIFF_HEADS, axis=1)])
    bias0 = _key_bias(rel_bias, past_len, jnp.array([past_len])).T
    o = diff_decode_attention(page_table, q2, cache, li, bias, per_map(kv[:, :DIFF_Q_COLS]),
                              per_map(kv[:, DIFF_Q_COLS:]), bias0, lam, g, lambda_init)
    return o.reshape(bsz, DIFF_Q_COLS), kv.reshape(bsz, 1, 2, DIFF_HEADS, DIFF_VD)


def _nsa_decode_kernel(pt_ref, cache_ref, qbd_ref, kvn_ref, gt_ref, win_ref, w1p_ref, posb_ref, w1_ref,
                       w2_ref, covt_ref, gsum_ref, eye_ref, expand_ref, kbias_ref, wbias_ref, b0_ref, o_ref,
                       bufc, bufs, slab, sc_ref, msk_ref, sem, *, li, n_pages, page, past_len):
    b = pl.program_id(0)
    n_heads = NSA_HEADS
    pc = NSA_PAGES_PER_CHUNK

    def page_copy(p, part):
        buf = bufs if part else bufc
        return pltpu.make_async_copy(cache_ref.at[li, pt_ref[b, p], pl.ds(2 * part, 2)], buf.at[p],
                                     sem.at[part])

    def start_all(p, _):
        page_copy(p, 0).start()
        page_copy(p, 1).start()
        return 0

    lax.fori_loop(0, n_pages, start_all, 0)

    def wait_part(part):
        def body(p, _):
            page_copy(p, part).wait()
            return 0
        lax.fori_loop(0, n_pages, body, 0)

    qbd = qbd_ref[0]
    q_pos = past_len
    n_chunk = past_len // CMP_STRIDE
    n_cmp = (past_len + 1 - CMP_BLOCK) // CMP_STRIDE + 1
    assert n_cmp + CMP_BLOCK // CMP_STRIDE - 1 <= n_chunk
    eye = eye_ref[...]
    hi = lax.Precision.HIGHEST

    wait_part(0)
    cmp = []
    for t in range(2):
        pos_term = _pos_term(posb_ref.at[t], w1_ref.at[t])
        outs = []
        for pair in range(NSA_GROUPS // 2):
            def fill(p, _):
                xt = bufc[p, t, pair * 2 * NSA_HD:(pair + 1) * 2 * NSA_HD, :]
                slab[pl.ds(pl.multiple_of(p * page, page), page), :] = xt.T
                return 0
            lax.fori_loop(0, n_pages, fill, 0, unroll=8)
            outs += _compress_pair(slab, n_chunk, w1p_ref.at[t], pos_term, w2_ref[t])
        cmp.append(outs)
    s_c = jnp.zeros((n_heads, n_chunk), F32)
    for g in range(NSA_GROUPS):
        s_c = s_c + lax.dot_general(qbd[:, g * NSA_HD:(g + 1) * NSA_HD], cmp[0][g].astype(BF16), _NT,
                                    preferred_element_type=F32)
    n = lax.broadcasted_iota(jnp.int32, s_c.shape, 1)
    mask_c = (n * CMP_STRIDE + CMP_BLOCK - 1 <= q_pos) & (n < n_cmp)
    s_c = jnp.where(mask_c, s_c, NEG)
    e = jnp.where(mask_c, jnp.exp2(s_c - jnp.max(s_c, axis=1, keepdims=True)), 0.0)
    p_c = e / jnp.maximum(jnp.sum(e, axis=1, keepdims=True), 1e-30)
    p_cb = p_c.astype(BF16)
    o_c = jnp.concatenate([jnp.dot(p_cb, cmp[1][g].astype(BF16), preferred_element_type=F32)
                           for g in range(NSA_GROUPS)], axis=1)
    p_ct = lax.dot_general(p_c, eye, _TN, precision=hi, preferred_element_type=F32)
    p_sum = jnp.dot(p_ct, gsum_ref[...], precision=hi, preferred_element_type=F32)
    imp_t = jnp.dot(covt_ref[...], p_sum, precision=hi, preferred_element_type=F32)

    qp_t = jnp.full(imp_t.shape, q_pos, jnp.int32)
    msk_ref[...] = _selection_mask_t(imp_t, qp_t, q_pos // SEL_BLOCK + 1, sc_ref)

    wait_part(1)
    blocks_per_chunk = pc * page // SEL_BLOCK

    def sel_chunk(c, carry, bias):
        m, l, acc = carry
        s = jnp.concatenate([jnp.dot(qbd, bufs[c * pc + k, 0].astype(BF16), preferred_element_type=F32)
                             for k in range(pc)], axis=1)
        blk = msk_ref[pl.ds(c * blocks_per_chunk, blocks_per_chunk), :]
        s = s + lax.dot_general(blk, expand_ref[...], _TN, preferred_element_type=F32)
        if bias is not None:
            s = s + bias
        m_new = jnp.maximum(m, jnp.max(s, axis=1, keepdims=True))
        pr = jnp.where(s > HALF_NEG, jnp.exp2(s - m_new), 0.0)
        a = jnp.exp2(m - m_new)
        l = a * l + jnp.sum(pr, axis=1, keepdims=True)
        pv = jnp.zeros(acc.shape, F32)
        for k in range(pc):
            pv = pv + lax.dot_general(pr[:, k * page:(k + 1) * page].astype(BF16),
                                      bufs[c * pc + k, 1].astype(BF16), _NT, preferred_element_type=F32)
        return m_new, l, a * acc + pv

    init = (jnp.full((n_heads, 1), NEG, F32), jnp.zeros((n_heads, 1), F32),
            jnp.zeros((n_heads, NSA_KV_COLS), F32))
    n_chunks = n_pages // pc
    carry = lax.fori_loop(0, n_chunks - 1, lambda c, cr: sel_chunk(c, cr, None), init)
    m, l, acc = sel_chunk(n_chunks - 1, carry, kbias_ref[...])
    kvn = kvn_ref[0]
    qbd_f = qbd.astype(F32)

    def new_row(k_row, v_row, extra, m, l, acc):
        s_new = jnp.sum(qbd_f * k_row, axis=1, keepdims=True) + extra
        m_new = jnp.maximum(m, s_new)
        p_new = jnp.where(s_new > HALF_NEG, jnp.exp2(s_new - m_new), 0.0)
        a = jnp.exp2(m - m_new)
        return a * l + p_new, a * acc + p_new * v_row

    new_block = jnp.broadcast_to(msk_ref[pl.ds(past_len // SEL_BLOCK, 1), :], eye.shape)
    new_block = jnp.sum(jnp.where(eye > 0.5, new_block, 0.0), axis=1, keepdims=True)
    l, acc = new_row(kvn[:, 2 * NSA_KV_COLS:3 * NSA_KV_COLS], kvn[:, 3 * NSA_KV_COLS:4 * NSA_KV_COLS],
                     b0_ref[...] + new_block, m, l, acc)
    o_s = acc / jnp.maximum(l, 1e-30)

    win = win_ref[0]
    s_w = lax.dot_general(qbd, win[:, 0:NSA_KV_COLS].astype(BF16), _NT, preferred_element_type=F32)
    s_w = s_w + wbias_ref[...]
    m_w = jnp.max(s_w, axis=1, keepdims=True)
    p_w = jnp.where(s_w > HALF_NEG, jnp.exp2(s_w - m_w), 0.0)
    l_w = jnp.sum(p_w, axis=1, keepdims=True)
    acc_w = jnp.dot(p_w.astype(BF16), win[:, NSA_KV_COLS:2 * NSA_KV_COLS].astype(BF16),
                    preferred_element_type=F32)
    l_w, acc_w = new_row(kvn[:, 4 * NSA_KV_COLS:5 * NSA_KV_COLS], kvn[:, 5 * NSA_KV_COLS:6 * NSA_KV_COLS],
                         b0_ref[...], m_w, l_w, acc_w)
    o_w = acc_w / jnp.maximum(l_w, 1e-30)

    gt = gt_ref[0]
    o_ref[0] = gt[:, 0:1] * o_c + gt[:, 1:2] * o_s + gt[:, 2:3] * o_w


def nsa_decode_attention(page_table, cache, li, qbd, kv_new, gates, win, cmp_w, cover_t, gsum, eye, expand,
                         key_bias, win_bias, bias0):
    bsz, n_pages = page_table.shape
    page = cache.shape[4]
    past_len = n_pages * page
    pc = NSA_PAGES_PER_CHUNK
    assert pc * page >= MAX_DISTANCE and page % SEL_BLOCK == 0 and n_pages % pc == 0
    w1p, posb, w1, w2 = cmp_w
    n_selp = cover_t.shape[0]
    full = lambda a: pl.BlockSpec(a.shape, lambda b, pt, _n=a.ndim: (0,) * _n)
    per_b = lambda a: pl.BlockSpec((1,) + a.shape[1:], lambda b, pt, _n=a.ndim: (b,) + (0,) * (_n - 1))
    grid_spec = pltpu.PrefetchScalarGridSpec(
        num_scalar_prefetch=1,
        grid=(bsz,),
        in_specs=[pl.BlockSpec(memory_space=pl.ANY), per_b(qbd), per_b(kv_new), per_b(gates),
                  per_b(win), full(w1p), full(posb), full(w1), full(w2), full(cover_t), full(gsum),
                  full(eye), full(expand), full(key_bias), full(win_bias), full(bias0)],
        out_specs=pl.BlockSpec((1, NSA_HEADS, NSA_KV_COLS), lambda b, pt: (b, 0, 0)),
        scratch_shapes=[pltpu.VMEM((n_pages, 2, NSA_KV_COLS, page), F32),
                        pltpu.VMEM((n_pages, 2, NSA_KV_COLS, page), F32),
                        pltpu.VMEM((past_len, 2 * NSA_HD), F32),
                        pltpu.VMEM((n_selp, NSA_HEADS), F32),
                        pltpu.VMEM((n_selp, NSA_HEADS), F32),
                        pltpu.SemaphoreType.DMA((2,))],
    )
    return pl.pallas_call(
        functools.partial(_nsa_decode_kernel, li=li, n_pages=n_pages, page=page, past_len=past_len),
        out_shape=jax.ShapeDtypeStruct((bsz, NSA_HEADS, NSA_KV_COLS), F32),
        grid_spec=grid_spec,
        compiler_params=_params(("arbitrary",)),
        name="nsa_decode_attention",
    )(page_table, cache, qbd, kv_new, gates, win, w1p, posb, w1, w2, cover_t, gsum, eye, expand, key_bias,
      win_bias, bias0)


def nsa_decode(x, wts, cache, li, win_state, page_table, rel_bias):
    w_q, w_kv, w_g, _, cmp_w = wts
    bsz = x.shape[0]
    n_pool, page = cache.shape[1], cache.shape[2]
    past_len = page_table.shape[1] * page
    wb = win_state.shape[1]
    x2 = x.reshape(bsz, D_MODEL)
    (q,) = linear(x2, w_q, (F32,))
    (kv,) = linear(x2, w_kv, (F32,))
    (gates,) = linear(x2, w_g, (F32,), act="sigmoid", tn=LANES)
    qh = q.reshape(bsz, NSA_GROUPS, NSA_HPG, NSA_HD)
    same_g = jnp.arange(NSA_GROUPS)[:, None] == jnp.arange(NSA_GROUPS)[None, :]
    qbd = jnp.where(same_g[None, :, None, :, None], qh[:, :, :, None, :], 0.0)
    qbd = qbd.reshape(bsz, NSA_HEADS, NSA_KV_COLS).astype(BF16)
    gates = gates[:, :3 * NSA_HEADS].reshape(bsz, 3, NSA_HEADS).transpose(0, 2, 1)

    t = past_len + 1
    n_cmp = (t - CMP_BLOCK) // CMP_STRIDE + 1
    n_chunk = past_len // CMP_STRIDE
    n_sel = -(-t // SEL_BLOCK)
    n_selp = -(-n_sel // LANES) * LANES
    cover_t = _cover_t(n_cmp, n_chunk, n_sel, n_selp)
    head = jnp.arange(NSA_HEADS)
    gsum = (head[:, None] // NSA_HPG == head[None, :] // NSA_HPG).astype(F32)
    eye = jnp.eye(NSA_HEADS, dtype=F32)
    n_keys = NSA_PAGES_PER_CHUNK * page
    expand = (jnp.arange(n_keys)[None, :] // SEL_BLOCK == jnp.arange(n_keys // SEL_BLOCK)[:, None]).astype(F32)
    key_bias = _key_bias(rel_bias, past_len, past_len - n_keys + jnp.arange(n_keys)).T
    dist_w = wb - jnp.arange(wb)
    win_bias = _key_bias(rel_bias, past_len, past_len - dist_w)
    win_bias = jnp.where((dist_w <= WINDOW)[:, None], win_bias, NEG).T
    bias0 = _key_bias(rel_bias, past_len, jnp.array([past_len])).T
    cache_t = jnp.transpose(cache, (0, 1, 3, 4, 5, 2)).reshape(cache.shape[0], n_pool, 4, NSA_KV_COLS, page)
    win2 = win_state.reshape(bsz, wb, 2 * NSA_KV_COLS)
    o = nsa_decode_attention(page_table, cache_t, li, qbd, kv[:, None, :], gates, win2, cmp_w, cover_t, gsum,
                             eye, expand, key_bias, win_bias, bias0)
    o = o.reshape(bsz, NSA_GROUPS, NSA_HPG, NSA_GROUPS, NSA_HD)
    o = jnp.stack([o[:, g, :, g, :] for g in range(NSA_GROUPS)], axis=1).reshape(bsz, NSA_Q_COLS)
    new_kv = kv[:, :4 * NSA_KV_COLS].reshape(bsz, 1, 4, NSA_GROUPS, NSA_HD)
    new_win = kv[:, 4 * NSA_KV_COLS:].reshape(bsz, 1, 2, NSA_GROUPS, NSA_HD)
    win_all = jnp.concatenate([win_state, new_win], axis=1)
    keep = min(WINDOW, wb + 1)
    return o, new_kv, win_all[:, wb + 1 - keep:]


def kernel(x_prompt, x_sample, cache_nsa_kv, state_nsa_win, cache_diff_kv, page_table, rel_bias,
           nsa_w_in, nsa_w_out, nsa_cmp_pos, nsa_cmp_w1, nsa_cmp_w2,
           diff_w_in, diff_w_out, diff_lambda, diff_subln_g,
           mlp_w_up, mlp_w_down, ln_g, ln_b):
    bp, sp, d = x_prompt.shape
    bs = x_sample.shape[0]
    xp = x_prompt.reshape(bp * sp, d)
    xs = x_sample.reshape(bs, d)
    nsa_bias = _tiles_t(_bias_tiles(rel_bias, NSA_TILE), NSA_HPG)
    diff_bias = _tiles_t(_bias_tiles(rel_bias, DIFF_TILE)[:2], 2)
    nsa_kv_p, nsa_kv_s, nsa_win_p, nsa_win_s, diff_kv_p, diff_kv_s = [], [], [], [], [], []
    for i in range(DEPTH):
        li = i // N_MIXERS
        if i % N_MIXERS == 0:
            wts = _nsa_weights(nsa_w_in[li], nsa_w_out[li], nsa_cmp_pos[li], nsa_cmp_w1[li], nsa_cmp_w2[li])
            op, kvp, wp = nsa_prompt(xp.reshape(bp, sp, d), wts, nsa_bias)
            os_, kvs, wsn = nsa_decode(xs.reshape(bs, 1, d), wts, cache_nsa_kv, li, state_nsa_win[li],
                                       page_table, rel_bias)
            nsa_kv_p.append(kvp)
            nsa_kv_s.append(kvs)
            nsa_win_p.append(wp)
            nsa_win_s.append(wsn)
            w_out = wts[3]
        else:
            lambda_init = 0.8 - 0.6 * math.exp(-0.3 * i)
            wts = _diff_weights(diff_w_in[li], diff_w_out[li])
            op, kvp = diff_prompt(xp.reshape(bp, sp, d), wts, diff_lambda[li], diff_subln_g[li], diff_bias,
                                  lambda_init)
            os_, kvs = diff_decode(xs.reshape(bs, 1, d), wts, cache_diff_kv, li, page_table, rel_bias,
                                   diff_lambda[li], diff_subln_g[li], lambda_init)
            diff_kv_p.append(kvp)
            diff_kv_s.append(kvs)
            w_out = wts[2]
        w_up, w_down = mlp_w_up[i].astype(BF16), mlp_w_down[i].astype(BF16)
        xp = linear_post_norm(op, w_out, xp, ln_g[i, 0], ln_b[i, 0])
        xs = linear_post_norm(os_, w_out, xs, ln_g[i, 0], ln_b[i, 0])
        xp = mlp_post_norm(xp, w_up, w_down, ln_g[i, 1], ln_b[i, 1])
        xs = mlp_post_norm(xs, w_up, w_down, ln_g[i, 1], ln_b[i, 1])
    return (xp.reshape(bp, sp, d), xs.reshape(bs, 1, d), jnp.stack(nsa_kv_p), jnp.stack(nsa_kv_s),
            jnp.stack(nsa_win_p), jnp.stack(nsa_win_s), jnp.stack(diff_kv_p), jnp.stack(diff_kv_s))
```

```python
import functools
import math

import jax
import jax.numpy as jnp
from jax import lax
from jax.experimental import pallas as pl
from jax.experimental.pallas import tpu as pltpu

F32 = jnp.float32
BF16 = jnp.bfloat16

D_MODEL = 1024
DEPTH = 4
N_MIXERS = 2
N_BUCKETS = 32
MAX_DISTANCE = 128
NSA_HEADS = 16
NSA_GROUPS = 4
NSA_HPG = NSA_HEADS // NSA_GROUPS
NSA_HD = D_MODEL // NSA_HEADS
CMP_BLOCK = 32
CMP_STRIDE = 16
CMP_HIDDEN = 2 * NSA_HD
SEL_BLOCK = 64
SEL_TOPK = 16
WINDOW = 512
FORCE_BONUS = 1e4
NSA_KV_COLS = NSA_GROUPS * NSA_HD
NSA_Q_COLS = NSA_HEADS * NSA_HD
DIFF_HEADS = 8
DIFF_D = D_MODEL // (2 * DIFF_HEADS)
DIFF_VD = 2 * DIFF_D
DIFF_Q_COLS = DIFF_HEADS * DIFF_VD
D_FF = 4 * D_MODEL
ALPHA = (2 * DEPTH) ** 0.25
LN_EPS = 1e-5
NEG = -1e30
HALF_NEG = -5e29

LOG2E = math.log2(math.e)
LANES = 128
NSA_TILE = 256
DIFF_TILE = 512
PAGES_PER_STEP = 8
NSA_PAGES_PER_CHUNK = 8
VMEM_LIMIT = 56 * 1024 * 1024

_NT = (((1,), (1,)), ((), ()))
_TN = (((0,), (0,)), ((), ()))


def _params(sem, vmem=VMEM_LIMIT):
    return pltpu.CompilerParams(dimension_semantics=sem, vmem_limit_bytes=vmem)


def _linear_kernel(a_ref, w_ref, *o_refs, act):
    y = jnp.dot(a_ref[...].astype(BF16), w_ref[...], preferred_element_type=F32)
    if act == "sigmoid":
        y = jax.nn.sigmoid(y)
    for o_ref in o_refs:
        o_ref[...] = y.astype(o_ref.dtype)


def linear(a, w, out_dtypes, act=None, tm=512, tn=2048):
    m, k = a.shape
    n = w.shape[1]
    tm, tn = min(tm, m), min(tn, n)
    assert m % tm == 0 and n % tn == 0
    outs = pl.pallas_call(
        functools.partial(_linear_kernel, act=act),
        out_shape=[jax.ShapeDtypeStruct((m, n), dt) for dt in out_dtypes],
        grid=(m // tm, n // tn),
        in_specs=[pl.BlockSpec((tm, k), lambda i, j: (i, 0)),
                  pl.BlockSpec((k, tn), lambda i, j: (0, j))],
        out_specs=[pl.BlockSpec((tm, tn), lambda i, j: (i, j)) for _ in out_dtypes],
        compiler_params=_params(("parallel", "parallel")),
        name="linear",
    )(a, w)
    return outs


def _post_norm(resid, sub, g, b):
    z = ALPHA * resid + sub
    mu = jnp.mean(z, axis=-1, keepdims=True)
    zc = z - mu
    var = jnp.mean(zc * zc, axis=-1, keepdims=True)
    return zc * lax.rsqrt(var + LN_EPS) * g + b


def _linear_ln_kernel(a_ref, w_ref, r_ref, g_ref, b_ref, o_ref):
    y = jnp.dot(a_ref[...].astype(BF16), w_ref[...], preferred_element_type=F32)
    o_ref[...] = _post_norm(r_ref[...], y, g_ref[...], b_ref[...])


def linear_post_norm(a, w, resid, g, b, tm=512):
    m, k = a.shape
    n = w.shape[1]
    tm = min(tm, m)
    assert m % tm == 0
    return pl.pallas_call(
        _linear_ln_kernel,
        out_shape=jax.ShapeDtypeStruct((m, n), F32),
        grid=(m // tm,),
        in_specs=[pl.BlockSpec((tm, k), lambda i: (i, 0)),
                  pl.BlockSpec((k, n), lambda i: (0, 0)),
                  pl.BlockSpec((tm, n), lambda i: (i, 0)),
                  pl.BlockSpec((1, n), lambda i: (0, 0)),
                  pl.BlockSpec((1, n), lambda i: (0, 0))],
        out_specs=pl.BlockSpec((tm, n), lambda i: (i, 0)),
        compiler_params=_params(("parallel",)),
        name="linear_post_norm",
    )(a, w, resid, g.reshape(1, n), b.reshape(1, n))


def _mlp_kernel(x_ref, wu_ref, wd_ref, g_ref, b_ref, o_ref, acc_ref):
    f = pl.program_id(1)

    @pl.when(f == 0)
    def _():
        acc_ref[...] = jnp.zeros_like(acc_ref)

    h = jnp.dot(x_ref[...].astype(BF16), wu_ref[...], preferred_element_type=F32)
    h = jnp.square(jnp.maximum(h, 0.0))
    acc_ref[...] += jnp.dot(h.astype(BF16), wd_ref[...], preferred_element_type=F32)

    @pl.when(f == pl.num_programs(1) - 1)
    def _():
        o_ref[...] = _post_norm(x_ref[...], acc_ref[...], g_ref[...], b_ref[...])


def mlp_post_norm(x, w_up, w_down, g, b, tm=1024, tf=1024):
    m, d = x.shape
    ff = w_up.shape[1]
    tm = min(tm, m)
    assert m % tm == 0 and ff % tf == 0
    return pl.pallas_call(
        _mlp_kernel,
        out_shape=jax.ShapeDtypeStruct((m, d), F32),
        grid=(m // tm, ff // tf),
        in_specs=[pl.BlockSpec((tm, d), lambda i, f: (i, 0)),
                  pl.BlockSpec((d, tf), lambda i, f: (0, f)),
                  pl.BlockSpec((tf, d), lambda i, f: (f, 0)),
                  pl.BlockSpec((1, d), lambda i, f: (0, 0)),
                  pl.BlockSpec((1, d), lambda i, f: (0, 0))],
        out_specs=pl.BlockSpec((tm, d), lambda i, f: (i, 0)),
        scratch_shapes=[pltpu.VMEM((tm, d), F32)],
        compiler_params=_params(("parallel", "arbitrary")),
        name="mlp_post_norm",
    )(x, w_up, w_down, g.reshape(1, d), b.reshape(1, d))


def _t5_bucket(dist):
    n = jnp.maximum(dist, 0)
    max_exact = N_BUCKETS // 2
    large = max_exact + (jnp.log(jnp.maximum(n, 1).astype(F32) / max_exact)
                         / math.log(MAX_DISTANCE / max_exact) * (N_BUCKETS - max_exact)).astype(jnp.int32)
    return jnp.where(n < max_exact, n, jnp.minimum(large, N_BUCKETS - 1))


def _bias_tiles(rel_bias, t):
    assert t >= MAX_DISTANCE
    r = jnp.arange(t)[:, None]
    c = jnp.arange(t)[None, :]

    def rel(off):
        k = jnp.arange(2 * t)
        dist = jnp.where(k < t, off - k, off + 2 * t - k)
        u = _key_bias(rel_bias, 0, -dist).T
        flat = jnp.tile(u, (1, t))[:, :t * (2 * t - 1)]
        return flat.reshape(-1, t, 2 * t - 1)[:, :, :t]

    prev = rel(t)
    diag = jnp.where((c <= r)[None], rel(0), NEG)
    edge = jnp.where((c >= r)[None], jnp.zeros_like(prev), NEG)
    return prev, diag, edge


def _tiles_t(tiles, maps_per_step):
    x = jnp.stack(tiles)
    k, maps, t, _ = x.shape
    x = x.reshape(k, maps // maps_per_step, maps_per_step, t, t)
    return jnp.transpose(x, (0, 1, 4, 2, 3)).reshape(k, maps // maps_per_step, t, maps_per_step * t)


def _key_bias(rel_bias, q_pos, k_pos):
    return (rel_bias[_t5_bucket(q_pos - k_pos)] - rel_bias[N_BUCKETS - 1][None, :]) * LOG2E


def _softmax_step(s, v, m, l, acc):
    m_new = jnp.maximum(m, jnp.max(s, axis=0, keepdims=True))
    p = jnp.exp2(s - m_new)
    a = jnp.exp2(m - m_new)
    l = a * l + jnp.sum(p, axis=0, keepdims=True)
    acc = a * acc + lax.dot_general(v, p.astype(BF16), _TN, preferred_element_type=F32)
    return m_new, l, acc


def _compress_pair(x_ref, n_chunk, w1_ref, pos_term, w2):
    acc = jnp.zeros((n_chunk, 4 * CMP_HIDDEN), F32)
    for s in range(0, CMP_STRIDE, 2):
        xs = jnp.concatenate([x_ref[pl.ds(s, n_chunk, stride=CMP_STRIDE), :],
                              x_ref[pl.ds(s + 1, n_chunk, stride=CMP_STRIDE), :]], axis=1)
        acc = acc + jnp.dot(xs.astype(BF16), w1_ref[s // 2], preferred_element_type=F32)
    outs = []
    for g2 in range(2):
        first = acc[:, g2 * 2 * CMP_HIDDEN: g2 * 2 * CMP_HIDDEN + CMP_HIDDEN]
        second = acc[:, g2 * 2 * CMP_HIDDEN + CMP_HIDDEN: (g2 + 1) * 2 * CMP_HIDDEN]
        second_next = jnp.concatenate([second[1:], second[:1]], axis=0)
        hid = first + second_next + pos_term
        outs.append(jnp.dot(jax.nn.gelu(hid).astype(BF16), w2, preferred_element_type=F32))
    return outs


def _compress_weights(pos, w1, w2):
    n_sub = CMP_BLOCK // CMP_STRIDE
    w1r = w1.reshape(2, n_sub, CMP_STRIDE, NSA_HD, CMP_HIDDEN)
    w1t = jnp.transpose(w1r, (0, 2, 3, 1, 4)).reshape(2, CMP_STRIDE, NSA_HD, n_sub * CMP_HIDDEN)
    zeros = jnp.zeros_like(w1t)
    w1p = jnp.concatenate([jnp.concatenate([w1t, zeros], -1), jnp.concatenate([zeros, w1t], -1)], axis=2)
    w1p = w1p.reshape(2, CMP_STRIDE // 2, 4 * NSA_HD, 2 * n_sub * CMP_HIDDEN)
    posb = jnp.broadcast_to(pos.reshape(2, 1, CMP_BLOCK * NSA_HD), (2, 8, CMP_BLOCK * NSA_HD))
    return w1p.astype(BF16), posb.astype(BF16), w1.astype(BF16), w2.astype(BF16)


def _pos_term(posb_ref, w1_ref):
    return jnp.dot(posb_ref[...], w1_ref[...], preferred_element_type=F32)[0:1]


def _compress_prompt_kernel(x_ref, w1p_ref, posb_ref, w1_ref, w2_ref, o_ref, *, n_chunk):
    pos_term = _pos_term(posb_ref.at[0], w1_ref.at[0])
    outs = _compress_pair(x_ref.at[0], n_chunk, w1p_ref.at[0], pos_term, w2_ref[0])
    for g2 in range(2):
        o_ref[0, 0, g2] = outs[g2].astype(o_ref.dtype)


def compress_prompt(kv, cmp_w):
    w1p, posb, w1, w2 = cmp_w
    bsz, s, _ = kv.shape
    n_chunk = s // CMP_STRIDE
    n_pair = NSA_GROUPS // 2
    return pl.pallas_call(
        functools.partial(_compress_prompt_kernel, n_chunk=n_chunk),
        out_shape=jax.ShapeDtypeStruct((2, bsz, NSA_GROUPS, n_chunk, NSA_HD), BF16),
        grid=(2, bsz, n_pair),
        in_specs=[pl.BlockSpec((1, s, 2 * NSA_HD), lambda t, b, pr: (b, 0, t * n_pair + pr)),
                  pl.BlockSpec((1,) + w1p.shape[1:], lambda t, b, pr: (t, 0, 0, 0)),
                  pl.BlockSpec((1,) + posb.shape[1:], lambda t, b, pr: (t, 0, 0)),
                  pl.BlockSpec((1,) + w1.shape[1:], lambda t, b, pr: (t, 0, 0)),
                  pl.BlockSpec((1,) + w2.shape[1:], lambda t, b, pr: (t, 0, 0))],
        out_specs=pl.BlockSpec((1, 1, 2, n_chunk, NSA_HD), lambda t, b, pr: (t, b, pr, 0, 0)),
        compiler_params=_params(("parallel", "parallel", "parallel")),
        name="nsa_compress_prompt",
    )(kv, w1p, posb, w1, w2)


def _selection_mask_t(imp_t, q_pos, n_valid_rows, sc_ref):
    j = lax.broadcasted_iota(jnp.int32, imp_t.shape, 0)
    cur = lax.shift_right_arithmetic(q_pos, SEL_BLOCK.bit_length() - 1)
    valid = j <= cur
    forced = (j == 0) | (j == cur) | (j == cur - 1)
    score = jnp.where(valid, imp_t + jnp.where(forced, FORCE_BONUS, 0.0), NEG)
    sc_ref[...] = score

    def body(jp, cnt):
        row = sc_ref[pl.ds(jp, 1), :]
        beats = (row > score) | ((row == score) & (jp < j))
        return cnt + jnp.where(beats, 1.0, 0.0)

    cnt = lax.fori_loop(0, n_valid_rows, body, jnp.zeros(imp_t.shape, F32))
    return jnp.where((cnt < SEL_TOPK) & valid, 0.0, NEG)


def _nsa_prompt_kernel(q_ref, kc_ref, vc_ref, ks_ref, vs_ref, kw_ref, vw_ref, oh_ref, gt_ref, cov_ref,
                       bias_ref, o_ref, sc_ref, *, n_cmp):
    t = NSA_TILE
    qi = pl.program_id(2)
    q0 = qi * t
    rows = NSA_HPG * t
    q = q_ref[0]
    q4 = jnp.concatenate([q[:, i * LANES:(i + 1) * LANES] for i in range(NSA_HPG)], axis=0)

    n_pad = kc_ref.shape[2]
    s_c = lax.dot_general(kc_ref[0, 0], q4, _NT, preferred_element_type=F32)
    q_pos = q0 + (lax.broadcasted_iota(jnp.int32, (n_pad, rows), 1) & (t - 1))
    n = lax.broadcasted_iota(jnp.int32, (n_pad, rows), 0)
    mask_c = (n * CMP_STRIDE + CMP_BLOCK - 1 <= q_pos) & (n < n_cmp)
    s_c = jnp.where(mask_c, s_c, NEG)
    m_c = jnp.max(s_c, axis=0, keepdims=True)
    e = jnp.exp2(s_c - m_c)
    inv = jnp.where(m_c > HALF_NEG, 1.0 / jnp.sum(e, axis=0, keepdims=True), 0.0)
    p_c = e * inv
    o_c = lax.dot_general(vc_ref[0, 0], p_c.astype(BF16), _TN, preferred_element_type=F32)
    p_sum = p_c[:, 0:t]
    for i in range(1, NSA_HPG):
        p_sum = p_sum + p_c[:, i * t:(i + 1) * t]
    imp_t = jnp.dot(cov_ref[...], p_sum, precision=lax.Precision.HIGHEST,
                    preferred_element_type=F32)

    n_sel = imp_t.shape[0]
    qp_t = q0 + lax.broadcasted_iota(jnp.int32, (n_sel, t), 1)
    msk_t = _selection_mask_t(imp_t, qp_t, (q0 + t - 1) // SEL_BLOCK + 1, sc_ref)
    pad = [jnp.zeros((LANES - NSA_HD - n_sel, t), F32)] if n_sel < LANES - NSA_HD else []
    msk = jnp.concatenate([jnp.zeros((NSA_HD, t), F32), msk_t] + pad, axis=0).T
    q_aug = q4 + jnp.concatenate([msk.astype(BF16)] * NSA_HPG, axis=0)

    init = (jnp.full((1, rows), NEG, F32), jnp.zeros((1, rows), F32), jnp.zeros((NSA_HD, rows), F32))
    prev_tile, diag_tile, edge_tile = 0, 1, 2

    def step(q, k_ref, v_ref, onehot, chunk, width, tile, carry):
        k0 = pl.multiple_of(chunk * t, t)
        k = k_ref[0, pl.ds(k0, width), :]
        if onehot:
            k = k + oh_ref[pl.ds(k0, width), :]
        s = lax.dot_general(k, q, _NT, preferred_element_type=F32)
        if tile is not None:
            s = s + bias_ref[tile, 0]
        return _softmax_step(s, v_ref[0, pl.ds(k0, width), 0:NSA_HD], *carry)

    def maybe(pred, fn, carry):
        return lax.cond(pred, fn, lambda cr: cr, carry)

    n_far = jnp.maximum(qi - 1, 0)
    sel = functools.partial(step, q_aug, ks_ref, vs_ref, True)
    carry = lax.fori_loop(0, n_far // 2, lambda c, cr: sel(2 * c, 2 * t, None, cr), init)
    carry = maybe(n_far % 2 == 1, lambda cr: sel(n_far - 1, t, None, cr), carry)
    carry = maybe(qi >= 1, lambda cr: sel(qi - 1, t, prev_tile, cr), carry)
    _, l_s, acc_s = sel(qi, t, diag_tile, carry)

    win = functools.partial(step, q4, kw_ref, vw_ref, False)
    carry = maybe(qi >= 2, lambda cr: win(qi - 2, t, edge_tile, cr), init)
    carry = maybe(qi >= 1, lambda cr: win(qi - 1, t, prev_tile, cr), carry)
    _, l_w, acc_w = win(qi, t, diag_tile, carry)

    gt = gt_ref[0, 0, 0]
    o = (gt[0:1] * o_c + (gt[1:2] / jnp.maximum(l_s, 1e-30)) * acc_s
         + (gt[2:3] / jnp.maximum(l_w, 1e-30)) * acc_w)
    halves = [jnp.concatenate([o[:, (2 * k) * t:(2 * k + 1) * t], o[:, (2 * k + 1) * t:(2 * k + 2) * t]],
                              axis=0).T for k in range(NSA_HPG // 2)]
    o_ref[0] = jnp.concatenate(halves, axis=1).astype(o_ref.dtype)


def nsa_prompt_attention(q, kv, kc, vc, onehot, gates, cover_t, bias, n_cmp):
    bsz, s, _ = q.shape
    t = NSA_TILE
    assert s % (2 * t) == 0 and WINDOW == 2 * t and NSA_HPG % 2 == 0
    n_sel = cover_t.shape[0]
    rows = NSA_HPG * t
    per_bg = lambda b, g, i: (b, g, 0, 0)
    kv_slot = lambda k: pl.BlockSpec((1, s, LANES), lambda b, g, i: (b, 0, k * NSA_GROUPS + g))
    return pl.pallas_call(
        functools.partial(_nsa_prompt_kernel, n_cmp=n_cmp),
        out_shape=jax.ShapeDtypeStruct((bsz, s, NSA_Q_COLS), BF16),
        grid=(bsz, NSA_GROUPS, s // t),
        in_specs=[pl.BlockSpec((1, t, NSA_HPG * LANES), lambda b, g, i: (b, i, g)),
                  pl.BlockSpec((1, 1) + kc.shape[2:], per_bg),
                  pl.BlockSpec((1, 1) + vc.shape[2:], per_bg),
                  kv_slot(0), kv_slot(1), kv_slot(2), kv_slot(3),
                  pl.BlockSpec(onehot.shape, lambda b, g, i: (0, 0)),
                  pl.BlockSpec((1, 1, 1, 3, rows), lambda b, g, i: (b, g, i, 0, 0)),
                  pl.BlockSpec(cover_t.shape, lambda b, g, i: (0, 0)),
                  pl.BlockSpec((3, 1, t, rows), lambda b, g, i: (0, g, 0, 0))],
        out_specs=pl.BlockSpec((1, t, NSA_HPG * NSA_HD), lambda b, g, i: (b, i, g)),
        scratch_shapes=[pltpu.VMEM((n_sel, t), F32)],
        compiler_params=_params(("parallel", "parallel", "arbitrary")),
        name="nsa_prompt_attention",
    )(q, kc, vc, kv, kv, kv, kv, onehot, gates, cover_t, bias)


def _cover_t(n_cmp, n_cmp_pad, n_sel, n_sel_pad):
    c_start = jnp.arange(n_cmp_pad)[None, :] * CMP_STRIDE
    s_start = jnp.arange(n_sel_pad)[:, None] * SEL_BLOCK
    cover = (c_start < s_start + SEL_BLOCK) & (c_start + CMP_BLOCK > s_start)
    cover = cover & (jnp.arange(n_cmp_pad)[None, :] < n_cmp) & (jnp.arange(n_sel_pad)[:, None] < n_sel)
    return cover.astype(F32)


def _nsa_weights(w_in, w_out, pos, w1, w2):
    scale = NSA_HD ** -0.5 * LOG2E
    kv0 = NSA_Q_COLS
    g0 = NSA_Q_COLS + 6 * NSA_KV_COLS
    w_q = (w_in[:, :kv0] * scale).astype(BF16)
    w_kv = w_in[:, kv0:g0].astype(BF16)
    w_g = jnp.pad(w_in[:, g0:], ((0, 0), (0, LANES - 3 * NSA_HEADS))).astype(BF16)
    slots = lambda w: jnp.pad(w.reshape(D_MODEL, -1, NSA_HD), ((0, 0), (0, 0), (0, LANES - NSA_HD))).reshape(D_MODEL, -1)
    w_attn = (slots(w_q), slots(w_kv[:, 2 * NSA_KV_COLS:]))
    return w_q, w_kv, w_g, w_out.astype(BF16), _compress_weights(pos, w1, w2), w_attn


def nsa_prompt(x, wts, bias):
    _, w_kv, w_g, _, cmp_w, (w_qs, w_kvs) = wts
    bsz, s, d = x.shape
    x2 = x.reshape(bsz * s, d)
    (q,) = linear(x2, w_qs, (BF16,))
    (kv32,) = linear(x2, w_kv, (F32,))
    (kv16,) = linear(x2, w_kvs, (BF16,))
    (gates,) = linear(x2, w_g, (F32,), act="sigmoid", tn=LANES)
    kv32 = kv32.reshape(bsz, s, 6 * NSA_KV_COLS)
    new_kv = kv32[..., :4 * NSA_KV_COLS].reshape(bsz, s, 4, NSA_GROUPS, NSA_HD)
    new_win = kv32[..., 4 * NSA_KV_COLS:].reshape(bsz, s, 2, NSA_GROUPS, NSA_HD)

    n_cmp = (s - CMP_BLOCK) // CMP_STRIDE + 1
    n_chunk = s // CMP_STRIDE
    n_sel = s // SEL_BLOCK
    assert n_sel <= LANES - NSA_HD
    cmp = compress_prompt(kv32, cmp_w)
    kc = jnp.pad(cmp[0], ((0, 0), (0, 0), (0, 0), (0, LANES - NSA_HD)))
    vc = cmp[1]
    onehot = (jnp.arange(s)[:, None] // SEL_BLOCK + NSA_HD == jnp.arange(LANES)[None, :]).astype(BF16)
    nq = s // NSA_TILE
    gates = gates[:, :3 * NSA_HEADS].reshape(bsz, nq, NSA_TILE, 3, NSA_GROUPS, NSA_HPG)
    gates = gates.transpose(0, 4, 1, 3, 5, 2).reshape(bsz, NSA_GROUPS, nq, 3, NSA_HPG * NSA_TILE)
    cover_t = _cover_t(n_cmp, n_chunk, n_sel, n_sel)
    o = nsa_prompt_attention(q.reshape(bsz, s, -1), kv16.reshape(bsz, s, -1), kc, vc, onehot, gates, cover_t,
                             bias, n_cmp)
    o = o.reshape(bsz * s, NSA_Q_COLS)
    keep = min(WINDOW, s)
    return o, new_kv, new_win[:, s - keep:]


def _lambda_full(lam_ref, lambda_init):
    lf = lam_ref[...]
    a = jnp.sum(lf[0:1] * lf[1:2], axis=-1, keepdims=True)
    b = jnp.sum(lf[2:3] * lf[3:4], axis=-1, keepdims=True)
    return jnp.exp(a) - jnp.exp(b) + lambda_init


def _sub_norm(o, g, lambda_init):
    return o * lax.rsqrt(jnp.mean(o * o, axis=-1, keepdims=True) + LN_EPS) * g * (1.0 - lambda_init)


def _diff_prompt_kernel(q_ref, k_ref, v_ref, bias_ref, lam_ref, g_ref, o_ref, *, lambda_init):
    t = DIFF_TILE
    qi = pl.program_id(2)
    q = q_ref[0]
    lane = lax.broadcasted_iota(jnp.int32, q.shape, 1)
    zero = jnp.zeros_like(q)
    q2 = jnp.concatenate([jnp.where(lane < DIFF_D, q, zero), jnp.where(lane >= DIFF_D, q, zero)], axis=0)
    rows = 2 * t
    init = (jnp.full((1, rows), NEG, F32), jnp.zeros((1, rows), F32), jnp.zeros((DIFF_VD, rows), F32))

    def step(chunk, tile, carry):
        k0 = pl.multiple_of(chunk * t, t)
        s = lax.dot_general(k_ref[0, pl.ds(k0, t), :], q2, _NT, preferred_element_type=F32)
        if tile is not None:
            s = s + bias_ref[tile, 0]
        return _softmax_step(s, v_ref[0, pl.ds(k0, t), :], *carry)

    carry = lax.fori_loop(0, jnp.maximum(qi - 1, 0), lambda c, cr: step(c, None, cr), init)
    carry = lax.cond(qi >= 1, lambda cr: step(qi - 1, 0, cr), lambda cr: cr, carry)
    _, l, acc = step(qi, 1, carry)
    o = acc / jnp.maximum(l, 1e-30)
    o = o[:, :t] - _lambda_full(lam_ref, lambda_init) * o[:, t:]
    o = o * lax.rsqrt(jnp.mean(o * o, axis=0, keepdims=True) + LN_EPS) * g_ref[...] * (1.0 - lambda_init)
    o_ref[0] = o.T.astype(o_ref.dtype)


def diff_prompt_attention(q, kv, bias, lam, g, lambda_init):
    bsz, s, _ = q.shape
    t = DIFF_TILE
    assert s % t == 0
    return pl.pallas_call(
        functools.partial(_diff_prompt_kernel, lambda_init=lambda_init),
        out_shape=jax.ShapeDtypeStruct((bsz, s, DIFF_Q_COLS), BF16),
        grid=(bsz, DIFF_HEADS, s // t),
        in_specs=[pl.BlockSpec((1, t, DIFF_VD), lambda b, h, i: (b, i, h)),
                  pl.BlockSpec((1, s, DIFF_VD), lambda b, h, i: (b, 0, h)),
                  pl.BlockSpec((1, s, DIFF_VD), lambda b, h, i: (b, 0, DIFF_HEADS + h)),
                  pl.BlockSpec((2, 1, t, 2 * t), lambda b, h, i: (0, h, 0, 0)),
                  pl.BlockSpec(lam.shape, lambda b, h, i: (0, 0)),
                  pl.BlockSpec((DIFF_VD, 1), lambda b, h, i: (0, 0))],
        out_specs=pl.BlockSpec((1, t, DIFF_VD), lambda b, h, i: (b, i, h)),
        compiler_params=_params(("parallel", "parallel", "arbitrary")),
        name="diff_prompt_attention",
    )(q, kv, kv, bias, lam, g.reshape(DIFF_VD, 1))


def _diff_weights(w_in, w_out):
    scale = DIFF_D ** -0.5 * LOG2E
    return (w_in[:, :DIFF_Q_COLS] * scale).astype(BF16), w_in[:, DIFF_Q_COLS:].astype(BF16), w_out.astype(BF16)


def diff_prompt(x, wts, lam, g, bias, lambda_init):
    w_q, w_kv, _ = wts
    bsz, s, d = x.shape
    x2 = x.reshape(bsz * s, d)
    (q,) = linear(x2, w_q, (BF16,))
    kv32, kv16 = linear(x2, w_kv, (F32, BF16))
    o = diff_prompt_attention(q.reshape(bsz, s, -1), kv16.reshape(bsz, s, -1), bias, lam, g, lambda_init)
    return o.reshape(bsz * s, DIFF_Q_COLS), kv32.reshape(bsz, s, 2, DIFF_HEADS, DIFF_VD)


def _diff_decode_kernel(pt_ref, q_ref, *refs, lambda_init, n_pg):
    c_refs = refs[:n_pg]
    bias_ref, kn_ref, vn_ref, b0_ref, lam_ref, g_ref, o_ref, m_ref, l_ref, acc_ref = refs[n_pg:]
    p = pl.program_id(1)
    last = pl.num_programs(1) - 1

    @pl.when(p == 0)
    def _():
        m_ref[...] = jnp.full_like(m_ref, NEG)
        l_ref[...] = jnp.zeros_like(l_ref)
        acc_ref[...] = jnp.zeros_like(acc_ref)

    q2 = q_ref[0]
    ks = [c_ref[:, 0].reshape(-1, DIFF_VD).astype(BF16) for c_ref in c_refs]
    vs = [c_ref[:, 1].reshape(-1, DIFF_VD).astype(BF16) for c_ref in c_refs]
    s = jnp.concatenate([lax.dot_general(q2, k, _NT, preferred_element_type=F32) for k in ks], axis=1)
    s = s + bias_ref[jnp.where(p == last, 1, 0)]
    m, l, acc = m_ref[...], l_ref[...], acc_ref[...]
    m_new = jnp.maximum(m, jnp.max(s, axis=-1, keepdims=True))
    pr = jnp.exp2(s - m_new)
    a = jnp.exp2(m - m_new)
    m, l = m_new, a * l + jnp.sum(pr, axis=-1, keepdims=True)
    acc = a * acc + jnp.dot(pr.astype(BF16), jnp.concatenate(vs, axis=0), preferred_element_type=F32)
    m_ref[...] = m
    l_ref[...] = l
    acc_ref[...] = acc

    @pl.when(p == last)
    def _():
        s_new = jnp.sum(q2.astype(F32) * kn_ref[0], axis=-1, keepdims=True) + b0_ref[...]
        m_new = jnp.maximum(m, s_new)
        a = jnp.exp2(m - m_new)
        p_new = jnp.exp2(s_new - m_new)
        o_n = (a * acc + p_new * vn_ref[0]) / jnp.maximum(a * l + p_new, 1e-30)
        lam_full = _lambda_full(lam_ref, lambda_init)
        for h in range(DIFF_HEADS):
            o_h = o_n[2 * h:2 * h + 1] - lam_full * o_n[2 * h + 1:2 * h + 2]
            o_ref[0, h:h + 1, :] = _sub_norm(o_h, g_ref[...], lambda_init)


def diff_decode_attention(page_table, q2, cache, li, bias, k_new, v_new, bias0, lam, g, lambda_init):
    bsz, n_pages = page_table.shape
    page = cache.shape[2]
    n_pg = PAGES_PER_STEP
    assert n_pages % n_pg == 0
    n_maps = 2 * DIFF_HEADS
    per_b = lambda b, p, pt: (b, 0, 0)

    def page_spec(k):
        return pl.BlockSpec((None, None, page, 2, DIFF_HEADS, DIFF_VD),
                            lambda b, p, pt: (li, pt[b, p * n_pg + k], 0, 0, 0, 0))

    grid_spec = pltpu.PrefetchScalarGridSpec(
        num_scalar_prefetch=1,
        grid=(bsz, n_pages // n_pg),
        in_specs=[pl.BlockSpec((1, n_maps, DIFF_VD), per_b)] + [page_spec(k) for k in range(n_pg)]
        + [pl.BlockSpec(bias.shape, lambda b, p, pt: (0, 0, 0)),
           pl.BlockSpec((1, n_maps, DIFF_VD), per_b),
           pl.BlockSpec((1, n_maps, DIFF_VD), per_b),
           pl.BlockSpec((n_maps, 1), lambda b, p, pt: (0, 0)),
           pl.BlockSpec(lam.shape, lambda b, p, pt: (0, 0)),
           pl.BlockSpec((1, DIFF_VD), lambda b, p, pt: (0, 0))],
        out_specs=pl.BlockSpec((1, DIFF_HEADS, DIFF_VD), per_b),
        scratch_shapes=[pltpu.VMEM((n_maps, 1), F32), pltpu.VMEM((n_maps, 1), F32),
                        pltpu.VMEM((n_maps, DIFF_VD), F32)],
    )
    return pl.pallas_call(
        functools.partial(_diff_decode_kernel, lambda_init=lambda_init, n_pg=n_pg),
        out_shape=jax.ShapeDtypeStruct((bsz, DIFF_HEADS, DIFF_VD), F32),
        grid_spec=grid_spec,
        compiler_params=_params(("parallel", "arbitrary")),
        name="diff_decode_attention",
    )(page_table, q2, *([cache] * n_pg), bias, k_new, v_new, bias0, lam, g.reshape(1, DIFF_VD))


def diff_decode(x, wts, cache, li, page_table, rel_bias, lam, g, lambda_init):
    w_q, w_kv, _ = wts
    bsz = x.shape[0]
    page = cache.shape[2]
    past_len = page_table.shape[1] * page
    x2 = x.reshape(bsz, D_MODEL)
    (q,) = linear(x2, w_q, (F32,))
    (kv,) = linear(x2, w_kv, (F32,))
    n_maps = 2 * DIFF_HEADS
    per_map = lambda a: jnp.repeat(a.reshape(bsz, DIFF_HEADS, DIFF_VD), 2, axis=1)
    own_lanes = jnp.arange(DIFF_VD)[None, :] // DIFF_D == jnp.arange(n_maps)[:, None] % 2
    q2 = jnp.where(own_lanes[None], per_map(q), 0.0).astype(BF16)
    n_keys = PAGES_PER_STEP * page
    assert n_keys >= MAX_DISTANCE
    col_head = jnp.arange(n_keys * DIFF_HEADS)[None, :] % DIFF_HEADS
    head_mask = jnp.where(col_head == jnp.arange(n_maps)[:, None] // 2, 0.0, NEG)
    near = _key_bias(rel_bias, past_len, past_len - n_keys + jnp.arange(n_keys))
    bias = jnp.stack([head_mask, head_mask + jnp.repeat(near.T, DIFF_HEADS, axis=1)])
    bias0 = _key_bias(rel_bias, past_len, jnp.array([past_len])).T
    o = diff_decode_attention(page_table, q2, cache, li, bias, per_map(kv[:, :DIFF_Q_COLS]),
                              per_map(kv[:, DIFF_Q_COLS:]), bias0, lam, g, lambda_init)
    return o.reshape(bsz, DIFF_Q_COLS), kv.reshape(bsz, 1, 2, DIFF_HEADS, DIFF_VD)


def _nsa_decode_kernel(pt_ref, cache_ref, qbd_ref, kvn_ref, gt_ref, win_ref, w1p_ref, posb_ref, w1_ref,
                       w2_ref, covt_ref, gsum_ref, eye_ref, expand_ref, kbias_ref, wbias_ref, b0_ref, o_ref,
                       bufc, bufs, slab, sc_ref, msk_ref, sem, *, li, n_pages, page, past_len):
    b = pl.program_id(0)
    n_heads = NSA_HEADS
    pc = NSA_PAGES_PER_CHUNK

    def page_copy(p, part):
        buf = bufs if part else bufc
        return pltpu.make_async_copy(cache_ref.at[li, pt_ref[b, p], pl.ds(2 * part, 2)], buf.at[p],
                                     sem.at[part])

    def start_all(p, _):
        page_copy(p, 0).start()
        page_copy(p, 1).start()
        return 0

    lax.fori_loop(0, n_pages, start_all, 0)

    def wait_part(part):
        def body(p, _):
            page_copy(p, part).wait()
            return 0
        lax.fori_loop(0, n_pages, body, 0)

    qbd = qbd_ref[0]
    q_pos = past_len
    n_chunk = past_len // CMP_STRIDE
    n_cmp = (past_len + 1 - CMP_BLOCK) // CMP_STRIDE + 1
    assert n_cmp + CMP_BLOCK // CMP_STRIDE - 1 <= n_chunk
    eye = eye_ref[...]
    hi = lax.Precision.HIGHEST

    wait_part(0)
    cmp = []
    for t in range(2):
        pos_term = _pos_term(posb_ref.at[t], w1_ref.at[t])
        outs = []
        for pair in range(NSA_GROUPS // 2):
            def fill(p, _):
                xt = bufc[p, t, pair * 2 * NSA_HD:(pair + 1) * 2 * NSA_HD, :]
                slab[pl.ds(pl.multiple_of(p * page, page), page), :] = xt.T
                return 0
            lax.fori_loop(0, n_pages, fill, 0, unroll=8)
            outs += _compress_pair(slab, n_chunk, w1p_ref.at[t], pos_term, w2_ref[t])
        cmp.append(outs)
    s_c = jnp.zeros((n_heads, n_chunk), F32)
    for g in range(NSA_GROUPS):
        s_c = s_c + lax.dot_general(qbd[:, g * NSA_HD:(g + 1) * NSA_HD], cmp[0][g].astype(BF16), _NT,
                                    preferred_element_type=F32)
    n = lax.broadcasted_iota(jnp.int32, s_c.shape, 1)
    mask_c = (n * CMP_STRIDE + CMP_BLOCK - 1 <= q_pos) & (n < n_cmp)
    s_c = jnp.where(mask_c, s_c, NEG)
    e = jnp.where(mask_c, jnp.exp2(s_c - jnp.max(s_c, axis=1, keepdims=True)), 0.0)
    p_c = e / jnp.maximum(jnp.sum(e, axis=1, keepdims=True), 1e-30)
    p_cb = p_c.astype(BF16)
    o_c = jnp.concatenate([jnp.dot(p_cb, cmp[1][g].astype(BF16), preferred_element_type=F32)
                           for g in range(NSA_GROUPS)], axis=1)
    p_ct = lax.dot_general(p_c, eye, _TN, precision=hi, preferred_element_type=F32)
    p_sum = jnp.dot(p_ct, gsum_ref[...], precision=hi, preferred_element_type=F32)
    imp_t = jnp.dot(covt_ref[...], p_sum, precision=hi, preferred_element_type=F32)

    qp_t = jnp.full(imp_t.shape, q_pos, jnp.int32)
    msk_ref[...] = _selection_mask_t(imp_t, qp_t, q_pos // SEL_BLOCK + 1, sc_ref)

    wait_part(1)
    blocks_per_chunk = pc * page // SEL_BLOCK

    def sel_chunk(c, carry, bias):
        m, l, acc = carry
        s = jnp.concatenate([jnp.dot(qbd, bufs[c * pc + k, 0].astype(BF16), preferred_element_type=F32)
                             for k in range(pc)], axis=1)
        blk = msk_ref[pl.ds(c * blocks_per_chunk, blocks_per_chunk), :]
        s = s + lax.dot_general(blk, expand_ref[...], _TN, preferred_element_type=F32)
        if bias is not None:
            s = s + bias
        m_new = jnp.maximum(m, jnp.max(s, axis=1, keepdims=True))
        pr = jnp.where(s > HALF_NEG, jnp.exp2(s - m_new), 0.0)
        a = jnp.exp2(m - m_new)
        l = a * l + jnp.sum(pr, axis=1, keepdims=True)
        pv = jnp.zeros(acc.shape, F32)
        for k in range(pc):
            pv = pv + lax.dot_general(pr[:, k * page:(k + 1) * page].astype(BF16),
                                      bufs[c * pc + k, 1].astype(BF16), _NT, preferred_element_type=F32)
        return m_new, l, a * acc + pv

    init = (jnp.full((n_heads, 1), NEG, F32), jnp.zeros((n_heads, 1), F32),
            jnp.zeros((n_heads, NSA_KV_COLS), F32))
    n_chunks = n_pages // pc
    carry = lax.fori_loop(0, n_chunks - 1, lambda c, cr: sel_chunk(c, cr, None), init)
    m, l, acc = sel_chunk(n_chunks - 1, carry, kbias_ref[...])
    kvn = kvn_ref[0]
    qbd_f = qbd.astype(F32)

    def new_row(k_row, v_row, extra, m, l, acc):
        s_new = jnp.sum(qbd_f * k_row, axis=1, keepdims=True) + extra
        m_new = jnp.maximum(m, s_new)
        p_new = jnp.where(s_new > HALF_NEG, jnp.exp2(s_new - m_new), 0.0)
        a = jnp.exp2(m - m_new)
        return a * l + p_new, a * acc + p_new * v_row

    new_block = jnp.broadcast_to(msk_ref[pl.ds(past_len // SEL_BLOCK, 1), :], eye.shape)
    new_block = jnp.sum(jnp.where(eye > 0.5, new_block, 0.0), axis=1, keepdims=True)
    l, acc = new_row(kvn[:, 2 * NSA_KV_COLS:3 * NSA_KV_COLS], kvn[:, 3 * NSA_KV_COLS:4 * NSA_KV_COLS],
                     b0_ref[...] + new_block, m, l, acc)
    o_s = acc / jnp.maximum(l, 1e-30)

    win = win_ref[0]
    s_w = lax.dot_general(qbd, win[:, 0:NSA_KV_COLS].astype(BF16), _NT, preferred_element_type=F32)
    s_w = s_w + wbias_ref[...]
    m_w = jnp.max(s_w, axis=1, keepdims=True)
    p_w = jnp.where(s_w > HALF_NEG, jnp.exp2(s_w - m_w), 0.0)
    l_w = jnp.sum(p_w, axis=1, keepdims=True)
    acc_w = jnp.dot(p_w.astype(BF16), win[:, NSA_KV_COLS:2 * NSA_KV_COLS].astype(BF16),
                    preferred_element_type=F32)
    l_w, acc_w = new_row(kvn[:, 4 * NSA_KV_COLS:5 * NSA_KV_COLS], kvn[:, 5 * NSA_KV_COLS:6 * NSA_KV_COLS],
                         b0_ref[...], m_w, l_w, acc_w)
    o_w = acc_w / jnp.maximum(l_w, 1e-30)

    gt = gt_ref[0]
    o_ref[0] = gt[:, 0:1] * o_c + gt[:, 1:2] * o_s + gt[:, 2:3] * o_w


def nsa_decode_attention(page_table, cache, li, qbd, kv_new, gates, win, cmp_w, cover_t, gsum, eye, expand,
                         key_bias, win_bias, bias0):
    bsz, n_pages = page_table.shape
    page = cache.shape[4]
    past_len = n_pages * page
    pc = NSA_PAGES_PER_CHUNK
    assert pc * page >= MAX_DISTANCE and page % SEL_BLOCK == 0 and n_pages % pc == 0
    w1p, posb, w1, w2 = cmp_w
    n_selp = cover_t.shape[0]
    full = lambda a: pl.BlockSpec(a.shape, lambda b, pt, _n=a.ndim: (0,) * _n)
    per_b = lambda a: pl.BlockSpec((1,) + a.shape[1:], lambda b, pt, _n=a.ndim: (b,) + (0,) * (_n - 1))
    grid_spec = pltpu.PrefetchScalarGridSpec(
        num_scalar_prefetch=1,
        grid=(bsz,),
        in_specs=[pl.BlockSpec(memory_space=pl.ANY), per_b(qbd), per_b(kv_new), per_b(gates),
                  per_b(win), full(w1p), full(posb), full(w1), full(w2), full(cover_t), full(gsum),
                  full(eye), full(expand), full(key_bias), full(win_bias), full(bias0)],
        out_specs=pl.BlockSpec((1, NSA_HEADS, NSA_KV_COLS), lambda b, pt: (b, 0, 0)),
        scratch_shapes=[pltpu.VMEM((n_pages, 2, NSA_KV_COLS, page), F32),
                        pltpu.VMEM((n_pages, 2, NSA_KV_COLS, page), F32),
                        pltpu.VMEM((past_len, 2 * NSA_HD), F32),
                        pltpu.VMEM((n_selp, NSA_HEADS), F32),
                        pltpu.VMEM((n_selp, NSA_HEADS), F32),
                        pltpu.SemaphoreType.DMA((2,))],
    )
    return pl.pallas_call(
        functools.partial(_nsa_decode_kernel, li=li, n_pages=n_pages, page=page, past_len=past_len),
        out_shape=jax.ShapeDtypeStruct((bsz, NSA_HEADS, NSA_KV_COLS), F32),
        grid_spec=grid_spec,
        compiler_params=_params(("arbitrary",)),
        name="nsa_decode_attention",
    )(page_table, cache, qbd, kv_new, gates, win, w1p, posb, w1, w2, cover_t, gsum, eye, expand, key_bias,
      win_bias, bias0)


def nsa_decode(x, wts, cache, li, win_state, page_table, rel_bias):
    w_q, w_kv, w_g, _, cmp_w, _ = wts
    bsz = x.shape[0]
    n_pool, page = cache.shape[1], cache.shape[2]
    past_len = page_table.shape[1] * page
    wb = win_state.shape[1]
    x2 = x.reshape(bsz, D_MODEL)
    (q,) = linear(x2, w_q, (F32,))
    (kv,) = linear(x2, w_kv, (F32,))
    (gates,) = linear(x2, w_g, (F32,), act="sigmoid", tn=LANES)
    qh = q.reshape(bsz, NSA_GROUPS, NSA_HPG, NSA_HD)
    same_g = jnp.arange(NSA_GROUPS)[:, None] == jnp.arange(NSA_GROUPS)[None, :]
    qbd = jnp.where(same_g[None, :, None, :, None], qh[:, :, :, None, :], 0.0)
    qbd = qbd.reshape(bsz, NSA_HEADS, NSA_KV_COLS).astype(BF16)
    gates = gates[:, :3 * NSA_HEADS].reshape(bsz, 3, NSA_HEADS).transpose(0, 2, 1)

    t = past_len + 1
    n_cmp = (t - CMP_BLOCK) // CMP_STRIDE + 1
    n_chunk = past_len // CMP_STRIDE
    n_sel = -(-t // SEL_BLOCK)
    n_selp = -(-n_sel // LANES) * LANES
    cover_t = _cover_t(n_cmp, n_chunk, n_sel, n_selp)
    head = jnp.arange(NSA_HEADS)
    gsum = (head[:, None] // NSA_HPG == head[None, :] // NSA_HPG).astype(F32)
    eye = jnp.eye(NSA_HEADS, dtype=F32)
    n_keys = NSA_PAGES_PER_CHUNK * page
    expand = (jnp.arange(n_keys)[None, :] // SEL_BLOCK == jnp.arange(n_keys // SEL_BLOCK)[:, None]).astype(F32)
    key_bias = _key_bias(rel_bias, past_len, past_len - n_keys + jnp.arange(n_keys)).T
    dist_w = wb - jnp.arange(wb)
    win_bias = _key_bias(rel_bias, past_len, past_len - dist_w)
    win_bias = jnp.where((dist_w <= WINDOW)[:, None], win_bias, NEG).T
    bias0 = _key_bias(rel_bias, past_len, jnp.array([past_len])).T
    cache_t = jnp.transpose(cache, (0, 1, 3, 4, 5, 2)).reshape(cache.shape[0], n_pool, 4, NSA_KV_COLS, page)
    win2 = win_state.reshape(bsz, wb, 2 * NSA_KV_COLS)
    o = nsa_decode_attention(page_table, cache_t, li, qbd, kv[:, None, :], gates, win2, cmp_w, cover_t, gsum,
                             eye, expand, key_bias, win_bias, bias0)
    o = o.reshape(bsz, NSA_GROUPS, NSA_HPG, NSA_GROUPS, NSA_HD)
    o = jnp.stack([o[:, g, :, g, :] for g in range(NSA_GROUPS)], axis=1).reshape(bsz, NSA_Q_COLS)
    new_kv = kv[:, :4 * NSA_KV_COLS].reshape(bsz, 1, 4, NSA_GROUPS, NSA_HD)
    new_win = kv[:, 4 * NSA_KV_COLS:].reshape(bsz, 1, 2, NSA_GROUPS, NSA_HD)
    win_all = jnp.concatenate([win_state, new_win], axis=1)
    keep = min(WINDOW, wb + 1)
    return o, new_kv, win_all[:, wb + 1 - keep:]


def kernel(x_prompt, x_sample, cache_nsa_kv, state_nsa_win, cache_diff_kv, page_table, rel_bias,
           nsa_w_in, nsa_w_out, nsa_cmp_pos, nsa_cmp_w1, nsa_cmp_w2,
           diff_w_in, diff_w_out, diff_lambda, diff_subln_g,
           mlp_w_up, mlp_w_down, ln_g, ln_b):
    bp, sp, d = x_prompt.shape
    bs = x_sample.shape[0]
    xp = x_prompt.reshape(bp * sp, d)
    xs = x_sample.reshape(bs, d)
    nsa_bias = _tiles_t(_bias_tiles(rel_bias, NSA_TILE), NSA_HPG)
    diff_bias = _tiles_t(_bias_tiles(rel_bias, DIFF_TILE)[:2], 2)
    nsa_kv_p, nsa_kv_s, nsa_win_p, nsa_win_s, diff_kv_p, diff_kv_s = [], [], [], [], [], []
    for i in range(DEPTH):
        li = i // N_MIXERS
        if i % N_MIXERS == 0:
            wts = _nsa_weights(nsa_w_in[li], nsa_w_out[li], nsa_cmp_pos[li], nsa_cmp_w1[li], nsa_cmp_w2[li])
            op, kvp, wp = nsa_prompt(xp.reshape(bp, sp, d), wts, nsa_bias)
            os_, kvs, wsn = nsa_decode(xs.reshape(bs, 1, d), wts, cache_nsa_kv, li, state_nsa_win[li],
                                       page_table, rel_bias)
            nsa_kv_p.append(kvp)
            nsa_kv_s.append(kvs)
            nsa_win_p.append(wp)
            nsa_win_s.append(wsn)
            w_out = wts[3]
        else:
            lambda_init = 0.8 - 0.6 * math.exp(-0.3 * i)
            wts = _diff_weights(diff_w_in[li], diff_w_out[li])
            op, kvp = diff_prompt(xp.reshape(bp, sp, d), wts, diff_lambda[li], diff_subln_g[li], diff_bias,
                                  lambda_init)
            os_, kvs = diff_decode(xs.reshape(bs, 1, d), wts, cache_diff_kv, li, page_table, rel_bias,
                                   diff_lambda[li], diff_subln_g[li], lambda_init)
            diff_kv_p.append(kvp)
            diff_kv_s.append(kvs)
            w_out = wts[2]
        w_up, w_down = mlp_w_up[i].astype(BF16), mlp_w_down[i].astype(BF16)
        xp = linear_post_norm(op, w_out, xp, ln_g[i, 0], ln_b[i, 0])
        xs = linear_post_norm(os_, w_out, xs, ln_g[i, 0], ln_b[i, 0])
        xp = mlp_post_norm(xp, w_up, w_down, ln_g[i, 1], ln_b[i, 1])
        xs = mlp_post_norm(xs, w_up, w_down, ln_g[i, 1], ln_b[i, 1])
    return (xp.reshape(bp, sp, d), xs.reshape(bs, 1, d), jnp.stack(nsa_kv_p), jnp.stack(nsa_kv_s),
            jnp.stack(nsa_win_p), jnp.stack(nsa_win_s), jnp.stack(diff_kv_p), jnp.stack(diff_kv_s))
```

```python
import functools
import math

import jax
import jax.numpy as jnp
from jax import lax
from jax.experimental import pallas as pl
from jax.experimental.pallas import tpu as pltpu

F32 = jnp.float32
BF16 = jnp.bfloat16

D_MODEL = 1024
DEPTH = 4
N_MIXERS = 2
N_BUCKETS = 32
MAX_DISTANCE = 128
NSA_HEADS = 16
NSA_GROUPS = 4
NSA_HPG = NSA_HEADS // NSA_GROUPS
NSA_HD = D_MODEL // NSA_HEADS
CMP_BLOCK = 32
CMP_STRIDE = 16
CMP_HIDDEN = 2 * NSA_HD
SEL_BLOCK = 64
SEL_TOPK = 16
WINDOW = 512
FORCE_BONUS = 1e4
NSA_KV_COLS = NSA_GROUPS * NSA_HD
NSA_Q_COLS = NSA_HEADS * NSA_HD
DIFF_HEADS = 8
DIFF_D = D_MODEL // (2 * DIFF_HEADS)
DIFF_VD = 2 * DIFF_D
DIFF_Q_COLS = DIFF_HEADS * DIFF_VD
D_FF = 4 * D_MODEL
ALPHA = (2 * DEPTH) ** 0.25
LN_EPS = 1e-5
NEG = -1e30
HALF_NEG = -5e29

LOG2E = math.log2(math.e)
LANES = 128
NSA_TILE = 256
DIFF_TILE = 512
PAGES_PER_STEP = 8
NSA_PAGES_PER_CHUNK = 8
VMEM_LIMIT = 56 * 1024 * 1024

_NT = (((1,), (1,)), ((), ()))
_TN = (((0,), (0,)), ((), ()))


def _params(sem, vmem=VMEM_LIMIT):
    return pltpu.CompilerParams(dimension_semantics=sem, vmem_limit_bytes=vmem)


def _linear_kernel(a_ref, w_ref, *o_refs, act):
    y = jnp.dot(a_ref[...].astype(BF16), w_ref[...], preferred_element_type=F32)
    if act == "sigmoid":
        y = jax.nn.sigmoid(y)
    for o_ref in o_refs:
        o_ref[...] = y.astype(o_ref.dtype)


def linear(a, w, out_dtypes, act=None, tm=512, tn=2048):
    m, k = a.shape
    n = w.shape[1]
    tm, tn = min(tm, m), min(tn, n)
    assert m % tm == 0 and n % tn == 0
    outs = pl.pallas_call(
        functools.partial(_linear_kernel, act=act),
        out_shape=[jax.ShapeDtypeStruct((m, n), dt) for dt in out_dtypes],
        grid=(m // tm, n // tn),
        in_specs=[pl.BlockSpec((tm, k), lambda i, j: (i, 0)),
                  pl.BlockSpec((k, tn), lambda i, j: (0, j))],
        out_specs=[pl.BlockSpec((tm, tn), lambda i, j: (i, j)) for _ in out_dtypes],
        compiler_params=_params(("parallel", "parallel")),
        name="linear",
    )(a, w)
    return outs


def _post_norm(resid, sub, g, b):
    z = ALPHA * resid + sub
    mu = jnp.mean(z, axis=-1, keepdims=True)
    zc = z - mu
    var = jnp.mean(zc * zc, axis=-1, keepdims=True)
    return zc * lax.rsqrt(var + LN_EPS) * g + b


def _linear_ln_kernel(a_ref, w_ref, r_ref, g_ref, b_ref, o_ref):
    y = jnp.dot(a_ref[...].astype(BF16), w_ref[...], preferred_element_type=F32)
    o_ref[...] = _post_norm(r_ref[...], y, g_ref[...], b_ref[...])


def linear_post_norm(a, w, resid, g, b, tm=512):
    m, k = a.shape
    n = w.shape[1]
    tm = min(tm, m)
    assert m % tm == 0
    return pl.pallas_call(
        _linear_ln_kernel,
        out_shape=jax.ShapeDtypeStruct((m, n), F32),
        grid=(m // tm,),
        in_specs=[pl.BlockSpec((tm, k), lambda i: (i, 0)),
                  pl.BlockSpec((k, n), lambda i: (0, 0)),
                  pl.BlockSpec((tm, n), lambda i: (i, 0)),
                  pl.BlockSpec((1, n), lambda i: (0, 0)),
                  pl.BlockSpec((1, n), lambda i: (0, 0))],
        out_specs=pl.BlockSpec((tm, n), lambda i: (i, 0)),
        compiler_params=_params(("parallel",)),
        name="linear_post_norm",
    )(a, w, resid, g.reshape(1, n), b.reshape(1, n))


def _mlp_kernel(x_ref, wu_ref, wd_ref, g_ref, b_ref, o_ref, acc_ref):
    f = pl.program_id(1)

    @pl.when(f == 0)
    def _():
        acc_ref[...] = jnp.zeros_like(acc_ref)

    h = jnp.dot(x_ref[...].astype(BF16), wu_ref[...], preferred_element_type=F32)
    h = jnp.square(jnp.maximum(h, 0.0))
    acc_ref[...] += jnp.dot(h.astype(BF16), wd_ref[...], preferred_element_type=F32)

    @pl.when(f == pl.num_programs(1) - 1)
    def _():
        o_ref[...] = _post_norm(x_ref[...], acc_ref[...], g_ref[...], b_ref[...])


def mlp_post_norm(x, w_up, w_down, g, b, tm=1024, tf=1024):
    m, d = x.shape
    ff = w_up.shape[1]
    tm = min(tm, m)
    assert m % tm == 0 and ff % tf == 0
    return pl.pallas_call(
        _mlp_kernel,
        out_shape=jax.ShapeDtypeStruct((m, d), F32),
        grid=(m // tm, ff // tf),
        in_specs=[pl.BlockSpec((tm, d), lambda i, f: (i, 0)),
                  pl.BlockSpec((d, tf), lambda i, f: (0, f)),
                  pl.BlockSpec((tf, d), lambda i, f: (f, 0)),
                  pl.BlockSpec((1, d), lambda i, f: (0, 0)),
                  pl.BlockSpec((1, d), lambda i, f: (0, 0))],
        out_specs=pl.BlockSpec((tm, d), lambda i, f: (i, 0)),
        scratch_shapes=[pltpu.VMEM((tm, d), F32)],
        compiler_params=_params(("parallel", "arbitrary")),
        name="mlp_post_norm",
    )(x, w_up, w_down, g.reshape(1, d), b.reshape(1, d))


def _t5_bucket(dist):
    n = jnp.maximum(dist, 0)
    max_exact = N_BUCKETS // 2
    large = max_exact + (jnp.log(jnp.maximum(n, 1).astype(F32) / max_exact)
                         / math.log(MAX_DISTANCE / max_exact) * (N_BUCKETS - max_exact)).astype(jnp.int32)
    return jnp.where(n < max_exact, n, jnp.minimum(large, N_BUCKETS - 1))


def _bias_tiles(rel_bias, t):
    assert t >= MAX_DISTANCE
    r = jnp.arange(t)[:, None]
    c = jnp.arange(t)[None, :]

    def rel(off):
        k = jnp.arange(2 * t)
        dist = jnp.where(k < t, off - k, off + 2 * t - k)
        u = _key_bias(rel_bias, 0, -dist).T
        flat = jnp.tile(u, (1, t))[:, :t * (2 * t - 1)]
        return flat.reshape(-1, t, 2 * t - 1)[:, :, :t]

    prev = rel(t)
    diag = jnp.where((c <= r)[None], rel(0), NEG)
    edge = jnp.where((c >= r)[None], jnp.zeros_like(prev), NEG)
    return prev, diag, edge


def _tiles_t(tiles, maps_per_step):
    x = jnp.stack(tiles)
    k, maps, t, _ = x.shape
    x = x.reshape(k, maps // maps_per_step, maps_per_step, t, t)
    return jnp.transpose(x, (0, 1, 4, 2, 3)).reshape(k, maps // maps_per_step, t, maps_per_step * t)


def _key_bias(rel_bias, q_pos, k_pos):
    return (rel_bias[_t5_bucket(q_pos - k_pos)] - rel_bias[N_BUCKETS - 1][None, :]) * LOG2E


def _softmax_step(s, v, m, l, acc):
    m_new = jnp.maximum(m, jnp.max(s, axis=0, keepdims=True))
    p = jnp.exp2(s - m_new)
    a = jnp.exp2(m - m_new)
    l = a * l + jnp.sum(p, axis=0, keepdims=True)
    acc = a * acc + lax.dot_general(v, p.astype(BF16), _TN, preferred_element_type=F32)
    return m_new, l, acc


def _compress_pair(x_ref, n_chunk, w1_ref, pos_term, w2):
    acc = jnp.zeros((n_chunk, 4 * CMP_HIDDEN), F32)
    for s in range(0, CMP_STRIDE, 2):
        xs = jnp.concatenate([x_ref[pl.ds(s, n_chunk, stride=CMP_STRIDE), :],
                              x_ref[pl.ds(s + 1, n_chunk, stride=CMP_STRIDE), :]], axis=1)
        acc = acc + jnp.dot(xs.astype(BF16), w1_ref[s // 2], preferred_element_type=F32)
    outs = []
    for g2 in range(2):
        first = acc[:, g2 * 2 * CMP_HIDDEN: g2 * 2 * CMP_HIDDEN + CMP_HIDDEN]
        second = acc[:, g2 * 2 * CMP_HIDDEN + CMP_HIDDEN: (g2 + 1) * 2 * CMP_HIDDEN]
        second_next = jnp.concatenate([second[1:], second[:1]], axis=0)
        hid = first + second_next + pos_term
        outs.append(jnp.dot(jax.nn.gelu(hid).astype(BF16), w2, preferred_element_type=F32))
    return outs


def _compress_weights(pos, w1, w2):
    n_sub = CMP_BLOCK // CMP_STRIDE
    w1r = w1.reshape(2, n_sub, CMP_STRIDE, NSA_HD, CMP_HIDDEN)
    w1t = jnp.transpose(w1r, (0, 2, 3, 1, 4)).reshape(2, CMP_STRIDE, NSA_HD, n_sub * CMP_HIDDEN)
    zeros = jnp.zeros_like(w1t)
    w1p = jnp.concatenate([jnp.concatenate([w1t, zeros], -1), jnp.concatenate([zeros, w1t], -1)], axis=2)
    w1p = w1p.reshape(2, CMP_STRIDE // 2, 4 * NSA_HD, 2 * n_sub * CMP_HIDDEN)
    posb = jnp.broadcast_to(pos.reshape(2, 1, CMP_BLOCK * NSA_HD), (2, 8, CMP_BLOCK * NSA_HD))
    return w1p.astype(BF16), posb.astype(BF16), w1.astype(BF16), w2.astype(BF16)


def _pos_term(posb_ref, w1_ref):
    return jnp.dot(posb_ref[...], w1_ref[...], preferred_element_type=F32)[0:1]


def _compress_prompt_kernel(x_ref, w1p_ref, posb_ref, w1_ref, w2_ref, o_ref, *, n_chunk):
    pos_term = _pos_term(posb_ref.at[0], w1_ref.at[0])
    outs = _compress_pair(x_ref.at[0], n_chunk, w1p_ref.at[0], pos_term, w2_ref[0])
    for g2 in range(2):
        o_ref[0, 0, g2] = outs[g2].astype(o_ref.dtype)


def compress_prompt(kv, cmp_w):
    w1p, posb, w1, w2 = cmp_w
    bsz, s, _ = kv.shape
    n_chunk = s // CMP_STRIDE
    n_pair = NSA_GROUPS // 2
    return pl.pallas_call(
        functools.partial(_compress_prompt_kernel, n_chunk=n_chunk),
        out_shape=jax.ShapeDtypeStruct((2, bsz, NSA_GROUPS, n_chunk, NSA_HD), BF16),
        grid=(2, bsz, n_pair),
        in_specs=[pl.BlockSpec((1, s, 2 * NSA_HD), lambda t, b, pr: (b, 0, t * n_pair + pr)),
                  pl.BlockSpec((1,) + w1p.shape[1:], lambda t, b, pr: (t, 0, 0, 0)),
                  pl.BlockSpec((1,) + posb.shape[1:], lambda t, b, pr: (t, 0, 0)),
                  pl.BlockSpec((1,) + w1.shape[1:], lambda t, b, pr: (t, 0, 0)),
                  pl.BlockSpec((1,) + w2.shape[1:], lambda t, b, pr: (t, 0, 0))],
        out_specs=pl.BlockSpec((1, 1, 2, n_chunk, NSA_HD), lambda t, b, pr: (t, b, pr, 0, 0)),
        compiler_params=_params(("parallel", "parallel", "parallel")),
        name="nsa_compress_prompt",
    )(kv, w1p, posb, w1, w2)


def _selection_mask_t(imp_t, q_pos, n_valid_rows, sc_ref):
    j = lax.broadcasted_iota(jnp.int32, imp_t.shape, 0)
    cur = lax.shift_right_arithmetic(q_pos, SEL_BLOCK.bit_length() - 1)
    valid = j <= cur
    forced = (j == 0) | (j == cur) | (j == cur - 1)
    score = jnp.where(valid, imp_t + jnp.where(forced, FORCE_BONUS, 0.0), NEG)
    sc_ref[...] = score

    def body(jp, cnt):
        row = sc_ref[pl.ds(jp, 1), :]
        beats = (row > score) | ((row == score) & (jp < j))
        return cnt + jnp.where(beats, 1.0, 0.0)

    cnt = lax.fori_loop(0, n_valid_rows, body, jnp.zeros(imp_t.shape, F32))
    return jnp.where((cnt < SEL_TOPK) & valid, 0.0, NEG)


def _nsa_prompt_kernel(q_ref, kc_ref, vc_ref, ks_ref, vs_ref, kw_ref, vw_ref, oh_ref, gt_ref, cov_ref,
                       bias_ref, o_ref, sc_ref, *, n_cmp):
    t = NSA_TILE
    qi = pl.program_id(2)
    q0 = qi * t
    rows = NSA_HPG * t
    q = q_ref[0]
    q4 = jnp.concatenate([q[:, i * LANES:(i + 1) * LANES] for i in range(NSA_HPG)], axis=0)

    n_pad = kc_ref.shape[2]
    s_c = lax.dot_general(kc_ref[0, 0], q4, _NT, preferred_element_type=F32)
    q_pos = q0 + (lax.broadcasted_iota(jnp.int32, (n_pad, rows), 1) & (t - 1))
    n = lax.broadcasted_iota(jnp.int32, (n_pad, rows), 0)
    mask_c = (n * CMP_STRIDE + CMP_BLOCK - 1 <= q_pos) & (n < n_cmp)
    s_c = jnp.where(mask_c, s_c, NEG)
    m_c = jnp.max(s_c, axis=0, keepdims=True)
    e = jnp.exp2(s_c - m_c)
    inv = jnp.where(m_c > HALF_NEG, 1.0 / jnp.sum(e, axis=0, keepdims=True), 0.0)
    p_c = e * inv
    o_c = lax.dot_general(vc_ref[0, 0], p_c.astype(BF16), _TN, preferred_element_type=F32)
    p_sum = p_c[:, 0:t]
    for i in range(1, NSA_HPG):
        p_sum = p_sum + p_c[:, i * t:(i + 1) * t]
    imp_t = jnp.dot(cov_ref[...], p_sum, precision=lax.Precision.HIGHEST,
                    preferred_element_type=F32)

    n_sel = imp_t.shape[0]
    qp_t = q0 + lax.broadcasted_iota(jnp.int32, (n_sel, t), 1)
    msk_t = _selection_mask_t(imp_t, qp_t, (q0 + t - 1) // SEL_BLOCK + 1, sc_ref)
    pad = [jnp.zeros((LANES - NSA_HD - n_sel, t), F32)] if n_sel < LANES - NSA_HD else []
    msk = jnp.concatenate([jnp.zeros((NSA_HD, t), F32), msk_t] + pad, axis=0).T
    q_aug = q4 + jnp.concatenate([msk.astype(BF16)] * NSA_HPG, axis=0)

    init = (jnp.full((1, rows), NEG, F32), jnp.zeros((1, rows), F32), jnp.zeros((NSA_HD, rows), F32))
    prev_tile, diag_tile, edge_tile = 0, 1, 2

    def step(q, k_ref, v_ref, onehot, chunk, width, tile, carry):
        k0 = pl.multiple_of(chunk * t, t)
        k = k_ref[0, pl.ds(k0, width), :]
        if onehot:
            k = k + oh_ref[pl.ds(k0, width), :]
        s = lax.dot_general(k, q, _NT, preferred_element_type=F32)
        if tile is not None:
            s = s + bias_ref[tile, 0]
        return _softmax_step(s, v_ref[0, pl.ds(k0, width), 0:NSA_HD], *carry)

    def maybe(pred, fn, carry):
        return lax.cond(pred, fn, lambda cr: cr, carry)

    n_far = jnp.maximum(qi - 1, 0)
    sel = functools.partial(step, q_aug, ks_ref, vs_ref, True)
    carry = lax.fori_loop(0, n_far // 2, lambda c, cr: sel(2 * c, 2 * t, None, cr), init)
    carry = maybe(n_far % 2 == 1, lambda cr: sel(n_far - 1, t, None, cr), carry)
    carry = maybe(qi >= 1, lambda cr: sel(qi - 1, t, prev_tile, cr), carry)
    _, l_s, acc_s = sel(qi, t, diag_tile, carry)

    win = functools.partial(step, q4, kw_ref, vw_ref, False)
    carry = maybe(qi >= 2, lambda cr: win(qi - 2, t, edge_tile, cr), init)
    carry = maybe(qi >= 1, lambda cr: win(qi - 1, t, prev_tile, cr), carry)
    _, l_w, acc_w = win(qi, t, diag_tile, carry)

    gt = gt_ref[0, 0, 0]
    o = (gt[0:1] * o_c + (gt[1:2] / jnp.maximum(l_s, 1e-30)) * acc_s
         + (gt[2:3] / jnp.maximum(l_w, 1e-30)) * acc_w)
    halves = [jnp.concatenate([o[:, (2 * k) * t:(2 * k + 1) * t], o[:, (2 * k + 1) * t:(2 * k + 2) * t]],
                              axis=0).T for k in range(NSA_HPG // 2)]
    o_ref[0] = jnp.concatenate(halves, axis=1).astype(o_ref.dtype)


def nsa_prompt_attention(q, kv, kc, vc, onehot, gates, cover_t, bias, n_cmp):
    bsz, s, _ = q.shape
    t = NSA_TILE
    assert s % (2 * t) == 0 and WINDOW == 2 * t and NSA_HPG % 2 == 0
    n_sel = cover_t.shape[0]
    rows = NSA_HPG * t
    per_bg = lambda b, g, i: (b, g, 0, 0)
    kv_slot = lambda k: pl.BlockSpec((1, s, LANES), lambda b, g, i: (b, 0, k * NSA_GROUPS + g))
    return pl.pallas_call(
        functools.partial(_nsa_prompt_kernel, n_cmp=n_cmp),
        out_shape=jax.ShapeDtypeStruct((bsz, s, NSA_Q_COLS), BF16),
        grid=(bsz, NSA_GROUPS, s // t),
        in_specs=[pl.BlockSpec((1, t, NSA_HPG * LANES), lambda b, g, i: (b, i, g)),
                  pl.BlockSpec((1, 1) + kc.shape[2:], per_bg),
                  pl.BlockSpec((1, 1) + vc.shape[2:], per_bg),
                  kv_slot(0), kv_slot(1), kv_slot(2), kv_slot(3),
                  pl.BlockSpec(onehot.shape, lambda b, g, i: (0, 0)),
                  pl.BlockSpec((1, 1, 1, 3, rows), lambda b, g, i: (b, g, i, 0, 0)),
                  pl.BlockSpec(cover_t.shape, lambda b, g, i: (0, 0)),
                  pl.BlockSpec((3, 1, t, rows), lambda b, g, i: (0, g, 0, 0))],
        out_specs=pl.BlockSpec((1, t, NSA_HPG * NSA_HD), lambda b, g, i: (b, i, g)),
        scratch_shapes=[pltpu.VMEM((n_sel, t), F32)],
        compiler_params=_params(("parallel", "parallel", "arbitrary")),
        name="nsa_prompt_attention",
    )(q, kc, vc, kv, kv, kv, kv, onehot, gates, cover_t, bias)


def _cover_t(n_cmp, n_cmp_pad, n_sel, n_sel_pad):
    c_start = jnp.arange(n_cmp_pad)[None, :] * CMP_STRIDE
    s_start = jnp.arange(n_sel_pad)[:, None] * SEL_BLOCK
    cover = (c_start < s_start + SEL_BLOCK) & (c_start + CMP_BLOCK > s_start)
    cover = cover & (jnp.arange(n_cmp_pad)[None, :] < n_cmp) & (jnp.arange(n_sel_pad)[:, None] < n_sel)
    return cover.astype(F32)


def _nsa_weights(w_in, w_out, pos, w1, w2):
    scale = NSA_HD ** -0.5 * LOG2E
    kv0 = NSA_Q_COLS
    g0 = NSA_Q_COLS + 6 * NSA_KV_COLS
    w_q = (w_in[:, :kv0] * scale).astype(BF16)
    w_kv = w_in[:, kv0:g0].astype(BF16)
    w_g = jnp.pad(w_in[:, g0:], ((0, 0), (0, LANES - 3 * NSA_HEADS))).astype(BF16)
    slots = lambda w: jnp.pad(w.reshape(D_MODEL, -1, NSA_HD), ((0, 0), (0, 0), (0, LANES - NSA_HD))).reshape(D_MODEL, -1)
    w_attn = (slots(w_q), slots(w_kv[:, 2 * NSA_KV_COLS:]))
    return w_q, w_kv, w_g, w_out.astype(BF16), _compress_weights(pos, w1, w2), w_attn


def nsa_prompt(x, wts, bias):
    _, w_kv, w_g, _, cmp_w, (w_qs, w_kvs) = wts
    bsz, s, d = x.shape
    x2 = x.reshape(bsz * s, d)
    (q,) = linear(x2, w_qs, (BF16,))
    (kv32,) = linear(x2, w_kv, (F32,))
    (kv16,) = linear(x2, w_kvs, (BF16,))
    (gates,) = linear(x2, w_g, (F32,), act="sigmoid", tn=LANES)
    kv32 = kv32.reshape(bsz, s, 6 * NSA_KV_COLS)
    new_kv = kv32[..., :4 * NSA_KV_COLS].reshape(bsz, s, 4, NSA_GROUPS, NSA_HD)
    new_win = kv32[..., 4 * NSA_KV_COLS:].reshape(bsz, s, 2, NSA_GROUPS, NSA_HD)

    n_cmp = (s - CMP_BLOCK) // CMP_STRIDE + 1
    n_chunk = s // CMP_STRIDE
    n_sel = s // SEL_BLOCK
    assert n_sel <= LANES - NSA_HD
    cmp = compress_prompt(kv32, cmp_w)
    kc = jnp.pad(cmp[0], ((0, 0), (0, 0), (0, 0), (0, LANES - NSA_HD)))
    vc = cmp[1]
    onehot = (jnp.arange(s)[:, None] // SEL_BLOCK + NSA_HD == jnp.arange(LANES)[None, :]).astype(BF16)
    nq = s // NSA_TILE
    gates = gates[:, :3 * NSA_HEADS].reshape(bsz, nq, NSA_TILE, 3, NSA_GROUPS, NSA_HPG)
    gates = gates.transpose(0, 4, 1, 3, 5, 2).reshape(bsz, NSA_GROUPS, nq, 3, NSA_HPG * NSA_TILE)
    cover_t = _cover_t(n_cmp, n_chunk, n_sel, n_sel)
    o = nsa_prompt_attention(q.reshape(bsz, s, -1), kv16.reshape(bsz, s, -1), kc, vc, onehot, gates, cover_t,
                             bias, n_cmp)
    o = o.reshape(bsz * s, NSA_Q_COLS)
    keep = min(WINDOW, s)
    return o, new_kv, new_win[:, s - keep:]


def _lambda_full(lam_ref, lambda_init):
    lf = lam_ref[...]
    a = jnp.sum(lf[0:1] * lf[1:2], axis=-1, keepdims=True)
    b = jnp.sum(lf[2:3] * lf[3:4], axis=-1, keepdims=True)
    return jnp.exp(a) - jnp.exp(b) + lambda_init


def _sub_norm(o, g, lambda_init):
    return o * lax.rsqrt(jnp.mean(o * o, axis=-1, keepdims=True) + LN_EPS) * g * (1.0 - lambda_init)


def _diff_prompt_kernel(q_ref, k_ref, v_ref, bias_ref, lam_ref, g_ref, o_ref, *, lambda_init):
    t = DIFF_TILE
    qi = pl.program_id(2)
    q = q_ref[0]
    lane = lax.broadcasted_iota(jnp.int32, q.shape, 1)
    zero = jnp.zeros_like(q)
    q2 = jnp.concatenate([jnp.where(lane < DIFF_D, q, zero), jnp.where(lane >= DIFF_D, q, zero)], axis=0)
    rows = 2 * t
    init = (jnp.full((1, rows), NEG, F32), jnp.zeros((1, rows), F32), jnp.zeros((DIFF_VD, rows), F32))

    def step(chunk, tile, carry):
        k0 = pl.multiple_of(chunk * t, t)
        s = lax.dot_general(k_ref[0, pl.ds(k0, t), :], q2, _NT, preferred_element_type=F32)
        if tile is not None:
            s = s + bias_ref[tile, 0]
        return _softmax_step(s, v_ref[0, pl.ds(k0, t), :], *carry)

    carry = lax.fori_loop(0, jnp.maximum(qi - 1, 0), lambda c, cr: step(c, None, cr), init)
    carry = lax.cond(qi >= 1, lambda cr: step(qi - 1, 0, cr), lambda cr: cr, carry)
    _, l, acc = step(qi, 1, carry)
    o = acc / jnp.maximum(l, 1e-30)
    o = o[:, :t] - _lambda_full(lam_ref, lambda_init) * o[:, t:]
    o = o * lax.rsqrt(jnp.mean(o * o, axis=0, keepdims=True) + LN_EPS) * g_ref[...] * (1.0 - lambda_init)
    o_ref[0] = o.T.astype(o_ref.dtype)


def diff_prompt_attention(q, kv, bias, lam, g, lambda_init):
    bsz, s, _ = q.shape
    t = DIFF_TILE
    assert s % t == 0
    return pl.pallas_call(
        functools.partial(_diff_prompt_kernel, lambda_init=lambda_init),
        out_shape=jax.ShapeDtypeStruct((bsz, s, DIFF_Q_COLS), BF16),
        grid=(bsz, DIFF_HEADS, s // t),
        in_specs=[pl.BlockSpec((1, t, DIFF_VD), lambda b, h, i: (b, i, h)),
                  pl.BlockSpec((1, s, DIFF_VD), lambda b, h, i: (b, 0, h)),
                  pl.BlockSpec((1, s, DIFF_VD), lambda b, h, i: (b, 0, DIFF_HEADS + h)),
                  pl.BlockSpec((2, 1, t, 2 * t), lambda b, h, i: (0, h, 0, 0)),
                  pl.BlockSpec(lam.shape, lambda b, h, i: (0, 0)),
                  pl.BlockSpec((DIFF_VD, 1), lambda b, h, i: (0, 0))],
        out_specs=pl.BlockSpec((1, t, DIFF_VD), lambda b, h, i: (b, i, h)),
        compiler_params=_params(("parallel", "parallel", "arbitrary")),
        name="diff_prompt_attention",
    )(q, kv, kv, bias, lam, g.reshape(DIFF_VD, 1))


def _diff_weights(w_in, w_out):
    scale = DIFF_D ** -0.5 * LOG2E
    return (w_in[:, :DIFF_Q_COLS] * scale).astype(BF16), w_in[:, DIFF_Q_COLS:].astype(BF16), w_out.astype(BF16)


def diff_prompt(x, wts, lam, g, bias, lambda_init):
    w_q, w_kv, _ = wts
    bsz, s, d = x.shape
    x2 = x.reshape(bsz * s, d)
    (q,) = linear(x2, w_q, (BF16,))
    kv32, kv16 = linear(x2, w_kv, (F32, BF16))
    o = diff_prompt_attention(q.reshape(bsz, s, -1), kv16.reshape(bsz, s, -1), bias, lam, g, lambda_init)
    return o.reshape(bsz * s, DIFF_Q_COLS), kv32.reshape(bsz, s, 2, DIFF_HEADS, DIFF_VD)


def _diff_decode_kernel(pt_ref, q_ref, *refs, lambda_init, n_pg):
    c_refs = refs[:n_pg]
    bias_ref, kn_ref, vn_ref, b0_ref, lam_ref, g_ref, o_ref, m_ref, l_ref, acc_ref = refs[n_pg:]
    p = pl.program_id(1)
    last = pl.num_programs(1) - 1

    @pl.when(p == 0)
    def _():
        m_ref[...] = jnp.full_like(m_ref, NEG)
        l_ref[...] = jnp.zeros_like(l_ref)
        acc_ref[...] = jnp.zeros_like(acc_ref)

    q2 = q_ref[0]
    ks = [c_ref[:, 0].reshape(-1, DIFF_VD).astype(BF16) for c_ref in c_refs]
    vs = [c_ref[:, 1].reshape(-1, DIFF_VD).astype(BF16) for c_ref in c_refs]
    s = jnp.concatenate([lax.dot_general(q2, k, _NT, preferred_element_type=F32) for k in ks], axis=1)
    s = s + bias_ref[jnp.where(p == last, 1, 0)]
    m, l, acc = m_ref[...], l_ref[...], acc_ref[...]
    m_new = jnp.maximum(m, jnp.max(s, axis=-1, keepdims=True))
    pr = jnp.exp2(s - m_new)
    a = jnp.exp2(m - m_new)
    m, l = m_new, a * l + jnp.sum(pr, axis=-1, keepdims=True)
    acc = a * acc + jnp.dot(pr.astype(BF16), jnp.concatenate(vs, axis=0), preferred_element_type=F32)
    m_ref[...] = m
    l_ref[...] = l
    acc_ref[...] = acc

    @pl.when(p == last)
    def _():
        s_new = jnp.sum(q2.astype(F32) * kn_ref[0], axis=-1, keepdims=True) + b0_ref[...]
        m_new = jnp.maximum(m, s_new)
        a = jnp.exp2(m - m_new)
        p_new = jnp.exp2(s_new - m_new)
        o_n = (a * acc + p_new * vn_ref[0]) / jnp.maximum(a * l + p_new, 1e-30)
        lam_full = _lambda_full(lam_ref, lambda_init)
        for h in range(DIFF_HEADS):
            o_h = o_n[2 * h:2 * h + 1] - lam_full * o_n[2 * h + 1:2 * h + 2]
            o_ref[0, h:h + 1, :] = _sub_norm(o_h, g_ref[...], lambda_init)


def diff_decode_attention(page_table, q2, cache, li, bias, k_new, v_new, bias0, lam, g, lambda_init):
    bsz, n_pages = page_table.shape
    page = cache.shape[2]
    n_pg = PAGES_PER_STEP
    assert n_pages % n_pg == 0
    n_maps = 2 * DIFF_HEADS
    per_b = lambda b, p, pt: (b, 0, 0)

    def page_spec(k):
        return pl.BlockSpec((None, None, page, 2, DIFF_HEADS, DIFF_VD),
                            lambda b, p, pt: (li, pt[b, p * n_pg + k], 0, 0, 0, 0))

    grid_spec = pltpu.PrefetchScalarGridSpec(
        num_scalar_prefetch=1,
        grid=(bsz, n_pages // n_pg),
        in_specs=[pl.BlockSpec((1, n_maps, DIFF_VD), per_b)] + [page_spec(k) for k in range(n_pg)]
        + [pl.BlockSpec(bias.shape, lambda b, p, pt: (0, 0, 0)),
           pl.BlockSpec((1, n_maps, DIFF_VD), per_b),
           pl.BlockSpec((1, n_maps, DIFF_VD), per_b),
           pl.BlockSpec((n_maps, 1), lambda b, p, pt: (0, 0)),
           pl.BlockSpec(lam.shape, lambda b, p, pt: (0, 0)),
           pl.BlockSpec((1, DIFF_VD), lambda b, p, pt: (0, 0))],
        out_specs=pl.BlockSpec((1, DIFF_HEADS, DIFF_VD), per_b),
        scratch_shapes=[pltpu.VMEM((n_maps, 1), F32), pltpu.VMEM((n_maps, 1), F32),
                        pltpu.VMEM((n_maps, DIFF_VD), F32)],
    )
    return pl.pallas_call(
        functools.partial(_diff_decode_kernel, lambda_init=lambda_init, n_pg=n_pg),
        out_shape=jax.ShapeDtypeStruct((bsz, DIFF_HEADS, DIFF_VD), F32),
        grid_spec=grid_spec,
        compiler_params=_params(("parallel", "arbitrary")),
        name="diff_decode_attention",
    )(page_table, q2, *([cache] * n_pg), bias, k_new, v_new, bias0, lam, g.reshape(1, DIFF_VD))


def diff_decode(x, wts, cache, li, page_table, rel_bias, lam, g, lambda_init):
    w_q, w_kv, _ = wts
    bsz = x.shape[0]
    page = cache.shape[2]
    past_len = page_table.shape[1] * page
    x2 = x.reshape(bsz, D_MODEL)
    (q,) = linear(x2, w_q, (F32,))
    (kv,) = linear(x2, w_kv, (F32,))
    n_maps = 2 * DIFF_HEADS
    per_map = lambda a: jnp.repeat(a.reshape(bsz, DIFF_HEADS, DIFF_VD), 2, axis=1)
    own_lanes = jnp.arange(DIFF_VD)[None, :] // DIFF_D == jnp.arange(n_maps)[:, None] % 2
    q2 = jnp.where(own_lanes[None], per_map(q), 0.0).astype(BF16)
    n_keys = PAGES_PER_STEP * page
    assert n_keys >= MAX_DISTANCE
    col_head = jnp.arange(n_keys * DIFF_HEADS)[None, :] % DIFF_HEADS
    head_mask = jnp.where(col_head == jnp.arange(n_maps)[:, None] // 2, 0.0, NEG)
    near = _key_bias(rel_bias, past_len, past_len - n_keys + jnp.arange(n_keys))
    bias = jnp.stack([head_mask, head_mask + jnp.repeat(near.T, DIFF_HEADS, axis=1)])
    bias0 = _key_bias(rel_bias, past_len, jnp.array([past_len])).T
    o = diff_decode_attention(page_table, q2, cache, li, bias, per_map(kv[:, :DIFF_Q_COLS]),
                              per_map(kv[:, DIFF_Q_COLS:]), bias0, lam, g, lambda_init)
    return o.reshape(bsz, DIFF_Q_COLS), kv.reshape(bsz, 1, 2, DIFF_HEADS, DIFF_VD)


def _nsa_decode_kernel(pt_ref, cache_ref, qbd_ref, kvn_ref, gt_ref, win_ref, w1p_ref, posb_ref, w1_ref,
                       w2_ref, covt_ref, gsum_ref, eye_ref, expand_ref, kbias_ref, wbias_ref, b0_ref, o_ref,
                       bufc, bufs, slab, sc_ref, msk_ref, sem, *, li, n_pages, page, past_len):
    b = pl.program_id(0)
    n_heads = NSA_HEADS
    pc = NSA_PAGES_PER_CHUNK

    def page_copy(p, part, seq):
        buf = bufs if part else bufc
        return pltpu.make_async_copy(cache_ref.at[li, pt_ref[seq, p], pl.ds(2 * part, 2)], buf.at[p],
                                     sem.at[part])

    def start_part(part, seq):
        def body(p, _):
            page_copy(p, part, seq).start()
            return 0
        lax.fori_loop(0, n_pages, body, 0)

    def wait_part(part):
        def body(p, _):
            page_copy(p, part, b).wait()
            return 0
        lax.fori_loop(0, n_pages, body, 0)

    @pl.when(b == 0)
    def _():
        start_part(0, b)

    start_part(1, b)

    qbd = qbd_ref[0]
    q_pos = past_len
    n_chunk = past_len // CMP_STRIDE
    n_cmp = (past_len + 1 - CMP_BLOCK) // CMP_STRIDE + 1
    assert n_cmp + CMP_BLOCK // CMP_STRIDE - 1 <= n_chunk
    eye = eye_ref[...]
    hi = lax.Precision.HIGHEST

    wait_part(0)
    cmp = []
    for t in range(2):
        pos_term = _pos_term(posb_ref.at[t], w1_ref.at[t])
        outs = []
        for pair in range(NSA_GROUPS // 2):
            def fill(p, _):
                xt = bufc[p, t, pair * 2 * NSA_HD:(pair + 1) * 2 * NSA_HD, :]
                slab[pl.ds(pl.multiple_of(p * page, page), page), :] = xt.T
                return 0
            lax.fori_loop(0, n_pages, fill, 0, unroll=8)
            if t == 1 and pair == NSA_GROUPS // 2 - 1:
                @pl.when(b + 1 < pl.num_programs(0))
                def _():
                    start_part(0, b + 1)
            outs += _compress_pair(slab, n_chunk, w1p_ref.at[t], pos_term, w2_ref[t])
        cmp.append(outs)
    s_c = jnp.zeros((n_heads, n_chunk), F32)
    for g in range(NSA_GROUPS):
        s_c = s_c + lax.dot_general(qbd[:, g * NSA_HD:(g + 1) * NSA_HD], cmp[0][g].astype(BF16), _NT,
                                    preferred_element_type=F32)
    n = lax.broadcasted_iota(jnp.int32, s_c.shape, 1)
    mask_c = (n * CMP_STRIDE + CMP_BLOCK - 1 <= q_pos) & (n < n_cmp)
    s_c = jnp.where(mask_c, s_c, NEG)
    e = jnp.where(mask_c, jnp.exp2(s_c - jnp.max(s_c, axis=1, keepdims=True)), 0.0)
    p_c = e / jnp.maximum(jnp.sum(e, axis=1, keepdims=True), 1e-30)
    p_cb = p_c.astype(BF16)
    o_c = jnp.concatenate([jnp.dot(p_cb, cmp[1][g].astype(BF16), preferred_element_type=F32)
                           for g in range(NSA_GROUPS)], axis=1)
    p_ct = lax.dot_general(p_c, eye, _TN, precision=hi, preferred_element_type=F32)
    p_sum = jnp.dot(p_ct, gsum_ref[...], precision=hi, preferred_element_type=F32)
    imp_t = jnp.dot(covt_ref[...], p_sum, precision=hi, preferred_element_type=F32)

    qp_t = jnp.full(imp_t.shape, q_pos, jnp.int32)
    msk_ref[...] = _selection_mask_t(imp_t, qp_t, q_pos // SEL_BLOCK + 1, sc_ref)

    wait_part(1)
    blocks_per_chunk = pc * page // SEL_BLOCK

    def sel_chunk(c, carry, bias):
        m, l, acc = carry
        s = jnp.concatenate([jnp.dot(qbd, bufs[c * pc + k, 0].astype(BF16), preferred_element_type=F32)
                             for k in range(pc)], axis=1)
        blk = msk_ref[pl.ds(c * blocks_per_chunk, blocks_per_chunk), :]
        s = s + lax.dot_general(blk, expand_ref[...], _TN, preferred_element_type=F32)
        if bias is not None:
            s = s + bias
        m_new = jnp.maximum(m, jnp.max(s, axis=1, keepdims=True))
        pr = jnp.where(s > HALF_NEG, jnp.exp2(s - m_new), 0.0)
        a = jnp.exp2(m - m_new)
        l = a * l + jnp.sum(pr, axis=1, keepdims=True)
        pv = jnp.zeros(acc.shape, F32)
        for k in range(pc):
            pv = pv + lax.dot_general(pr[:, k * page:(k + 1) * page].astype(BF16),
                                      bufs[c * pc + k, 1].astype(BF16), _NT, preferred_element_type=F32)
        return m_new, l, a * acc + pv

    init = (jnp.full((n_heads, 1), NEG, F32), jnp.zeros((n_heads, 1), F32),
            jnp.zeros((n_heads, NSA_KV_COLS), F32))
    n_chunks = n_pages // pc
    carry = lax.fori_loop(0, n_chunks - 1, lambda c, cr: sel_chunk(c, cr, None), init)
    m, l, acc = sel_chunk(n_chunks - 1, carry, kbias_ref[...])
    kvn = kvn_ref[0]
    qbd_f = qbd.astype(F32)

    def new_row(k_row, v_row, extra, m, l, acc):
        s_new = jnp.sum(qbd_f * k_row, axis=1, keepdims=True) + extra
        m_new = jnp.maximum(m, s_new)
        p_new = jnp.where(s_new > HALF_NEG, jnp.exp2(s_new - m_new), 0.0)
        a = jnp.exp2(m - m_new)
        return a * l + p_new, a * acc + p_new * v_row

    new_block = jnp.broadcast_to(msk_ref[pl.ds(past_len // SEL_BLOCK, 1), :], eye.shape)
    new_block = jnp.sum(jnp.where(eye > 0.5, new_block, 0.0), axis=1, keepdims=True)
    l, acc = new_row(kvn[:, 2 * NSA_KV_COLS:3 * NSA_KV_COLS], kvn[:, 3 * NSA_KV_COLS:4 * NSA_KV_COLS],
                     b0_ref[...] + new_block, m, l, acc)
    o_s = acc / jnp.maximum(l, 1e-30)

    win = win_ref[0]
    s_w = lax.dot_general(qbd, win[:, 0:NSA_KV_COLS].astype(BF16), _NT, preferred_element_type=F32)
    s_w = s_w + wbias_ref[...]
    m_w = jnp.max(s_w, axis=1, keepdims=True)
    p_w = jnp.where(s_w > HALF_NEG, jnp.exp2(s_w - m_w), 0.0)
    l_w = jnp.sum(p_w, axis=1, keepdims=True)
    acc_w = jnp.dot(p_w.astype(BF16), win[:, NSA_KV_COLS:2 * NSA_KV_COLS].astype(BF16),
                    preferred_element_type=F32)
    l_w, acc_w = new_row(kvn[:, 4 * NSA_KV_COLS:5 * NSA_KV_COLS], kvn[:, 5 * NSA_KV_COLS:6 * NSA_KV_COLS],
                         b0_ref[...], m_w, l_w, acc_w)
    o_w = acc_w / jnp.maximum(l_w, 1e-30)

    gt = gt_ref[0]
    o_ref[0] = gt[:, 0:1] * o_c + gt[:, 1:2] * o_s + gt[:, 2:3] * o_w


def nsa_decode_attention(page_table, cache, li, qbd, kv_new, gates, win, cmp_w, cover_t, gsum, eye, expand,
                         key_bias, win_bias, bias0):
    bsz, n_pages = page_table.shape
    page = cache.shape[4]
    past_len = n_pages * page
    pc = NSA_PAGES_PER_CHUNK
    assert pc * page >= MAX_DISTANCE and page % SEL_BLOCK == 0 and n_pages % pc == 0
    w1p, posb, w1, w2 = cmp_w
    n_selp = cover_t.shape[0]
    full = lambda a: pl.BlockSpec(a.shape, lambda b, pt, _n=a.ndim: (0,) * _n)
    per_b = lambda a: pl.BlockSpec((1,) + a.shape[1:], lambda b, pt, _n=a.ndim: (b,) + (0,) * (_n - 1))
    grid_spec = pltpu.PrefetchScalarGridSpec(
        num_scalar_prefetch=1,
        grid=(bsz,),
        in_specs=[pl.BlockSpec(memory_space=pl.ANY), per_b(qbd), per_b(kv_new), per_b(gates),
                  per_b(win), full(w1p), full(posb), full(w1), full(w2), full(cover_t), full(gsum),
                  full(eye), full(expand), full(key_bias), full(win_bias), full(bias0)],
        out_specs=pl.BlockSpec((1, NSA_HEADS, NSA_KV_COLS), lambda b, pt: (b, 0, 0)),
        scratch_shapes=[pltpu.VMEM((n_pages, 2, NSA_KV_COLS, page), F32),
                        pltpu.VMEM((n_pages, 2, NSA_KV_COLS, page), F32),
                        pltpu.VMEM((past_len, 2 * NSA_HD), F32),
                        pltpu.VMEM((n_selp, NSA_HEADS), F32),
                        pltpu.VMEM((n_selp, NSA_HEADS), F32),
                        pltpu.SemaphoreType.DMA((2,))],
    )
    return pl.pallas_call(
        functools.partial(_nsa_decode_kernel, li=li, n_pages=n_pages, page=page, past_len=past_len),
        out_shape=jax.ShapeDtypeStruct((bsz, NSA_HEADS, NSA_KV_COLS), F32),
        grid_spec=grid_spec,
        compiler_params=_params(("arbitrary",)),
        name="nsa_decode_attention",
    )(page_table, cache, qbd, kv_new, gates, win, w1p, posb, w1, w2, cover_t, gsum, eye, expand, key_bias,
      win_bias, bias0)


def nsa_decode(x, wts, cache, li, win_state, page_table, rel_bias):
    w_q, w_kv, w_g, _, cmp_w, _ = wts
    bsz = x.shape[0]
    n_pool, page = cache.shape[1], cache.shape[2]
    past_len = page_table.shape[1] * page
    wb = win_state.shape[1]
    x2 = x.reshape(bsz, D_MODEL)
    (q,) = linear(x2, w_q, (F32,))
    (kv,) = linear(x2, w_kv, (F32,))
    (gates,) = linear(x2, w_g, (F32,), act="sigmoid", tn=LANES)
    qh = q.reshape(bsz, NSA_GROUPS, NSA_HPG, NSA_HD)
    same_g = jnp.arange(NSA_GROUPS)[:, None] == jnp.arange(NSA_GROUPS)[None, :]
    qbd = jnp.where(same_g[None, :, None, :, None], qh[:, :, :, None, :], 0.0)
    qbd = qbd.reshape(bsz, NSA_HEADS, NSA_KV_COLS).astype(BF16)
    gates = gates[:, :3 * NSA_HEADS].reshape(bsz, 3, NSA_HEADS).transpose(0, 2, 1)

    t = past_len + 1
    n_cmp = (t - CMP_BLOCK) // CMP_STRIDE + 1
    n_chunk = past_len // CMP_STRIDE
    n_sel = -(-t // SEL_BLOCK)
    n_selp = -(-n_sel // LANES) * LANES
    cover_t = _cover_t(n_cmp, n_chunk, n_sel, n_selp)
    head = jnp.arange(NSA_HEADS)
    gsum = (head[:, None] // NSA_HPG == head[None, :] // NSA_HPG).astype(F32)
    eye = jnp.eye(NSA_HEADS, dtype=F32)
    n_keys = NSA_PAGES_PER_CHUNK * page
    expand = (jnp.arange(n_keys)[None, :] // SEL_BLOCK == jnp.arange(n_keys // SEL_BLOCK)[:, None]).astype(F32)
    key_bias = _key_bias(rel_bias, past_len, past_len - n_keys + jnp.arange(n_keys)).T
    dist_w = wb - jnp.arange(wb)
    win_bias = _key_bias(rel_bias, past_len, past_len - dist_w)
    win_bias = jnp.where((dist_w <= WINDOW)[:, None], win_bias, NEG).T
    bias0 = _key_bias(rel_bias, past_len, jnp.array([past_len])).T
    cache_t = jnp.transpose(cache, (0, 1, 3, 4, 5, 2)).reshape(cache.shape[0], n_pool, 4, NSA_KV_COLS, page)
    win2 = win_state.reshape(bsz, wb, 2 * NSA_KV_COLS)
    o = nsa_decode_attention(page_table, cache_t, li, qbd, kv[:, None, :], gates, win2, cmp_w, cover_t, gsum,
                             eye, expand, key_bias, win_bias, bias0)
    o = o.reshape(bsz, NSA_GROUPS, NSA_HPG, NSA_GROUPS, NSA_HD)
    o = jnp.stack([o[:, g, :, g, :] for g in range(NSA_GROUPS)], axis=1).reshape(bsz, NSA_Q_COLS)
    new_kv = kv[:, :4 * NSA_KV_COLS].reshape(bsz, 1, 4, NSA_GROUPS, NSA_HD)
    new_win = kv[:, 4 * NSA_KV_COLS:].reshape(bsz, 1, 2, NSA_GROUPS, NSA_HD)
    win_all = jnp.concatenate([win_state, new_win], axis=1)
    keep = min(WINDOW, wb + 1)
    return o, new_kv, win_all[:, wb + 1 - keep:]


def kernel(x_prompt, x_sample, cache_nsa_kv, state_nsa_win, cache_diff_kv, page_table, rel_bias,
           nsa_w_in, nsa_w_out, nsa_cmp_pos, nsa_cmp_w1, nsa_cmp_w2,
           diff_w_in, diff_w_out, diff_lambda, diff_subln_g,
           mlp_w_up, mlp_w_down, ln_g, ln_b):
    bp, sp, d = x_prompt.shape
    bs = x_sample.shape[0]
    xp = x_prompt.reshape(bp * sp, d)
    xs = x_sample.reshape(bs, d)
    nsa_bias = _tiles_t(_bias_tiles(rel_bias, NSA_TILE), NSA_HPG)
    diff_bias = _tiles_t(_bias_tiles(rel_bias, DIFF_TILE)[:2], 2)
    nsa_kv_p, nsa_kv_s, nsa_win_p, nsa_win_s, diff_kv_p, diff_kv_s = [], [], [], [], [], []
    for i in range(DEPTH):
        li = i // N_MIXERS
        if i % N_MIXERS == 0:
            wts = _nsa_weights(nsa_w_in[li], nsa_w_out[li], nsa_cmp_pos[li], nsa_cmp_w1[li], nsa_cmp_w2[li])
            op, kvp, wp = nsa_prompt(xp.reshape(bp, sp, d), wts, nsa_bias)
            os_, kvs, wsn = nsa_decode(xs.reshape(bs, 1, d), wts, cache_nsa_kv, li, state_nsa_win[li],
                                       page_table, rel_bias)
            nsa_kv_p.append(kvp)
            nsa_kv_s.append(kvs)
            nsa_win_p.append(wp)
            nsa_win_s.append(wsn)
            w_out = wts[3]
        else:
            lambda_init = 0.8 - 0.6 * math.exp(-0.3 * i)
            wts = _diff_weights(diff_w_in[li], diff_w_out[li])
            op, kvp = diff_prompt(xp.reshape(bp, sp, d), wts, diff_lambda[li], diff_subln_g[li], diff_bias,
                                  lambda_init)
            os_, kvs = diff_decode(xs.reshape(bs, 1, d), wts, cache_diff_kv, li, page_table, rel_bias,
                                   diff_lambda[li], diff_subln_g[li], lambda_init)
            diff_kv_p.append(kvp)
            diff_kv_s.append(kvs)
            w_out = wts[2]
        w_up, w_down = mlp_w_up[i].astype(BF16), mlp_w_down[i].astype(BF16)
        xp = linear_post_norm(op, w_out, xp, ln_g[i, 0], ln_b[i, 0])
        xs = linear_post_norm(os_, w_out, xs, ln_g[i, 0], ln_b[i, 0])
        xp = mlp_post_norm(xp, w_up, w_down, ln_g[i, 1], ln_b[i, 1])
        xs = mlp_post_norm(xs, w_up, w_down, ln_g[i, 1], ln_b[i, 1])
    return (xp.reshape(bp, sp, d), xs.reshape(bs, 1, d), jnp.stack(nsa_kv_p), jnp.stack(nsa_kv_s),
            jnp.stack(nsa_win_p), jnp.stack(nsa_win_s), jnp.stack(diff_kv_p), jnp.stack(diff_kv_s))
```
